```python
import math
import jax, jax.numpy as jnp
from jax import lax
import numpy as np

D_MODEL = 2048
BATCH = 4
SEQ = 2048
DEPTH = 4
DEC_BATCH = 8
DEC_SEQ = 1
PAST_LEN = 16384
PAGE_SIZE = 128

WA = D_MODEL // 2
DV_A = 128
HA = WA // DV_A
QK_HALF = DV_A // 2
WB = D_MODEL // 2
HB = 4
DK = WB // HB
DV_B = WB // HB
MLSTM_CHUNK = 64
D_FF = ((8 * D_MODEL + 3 * 256 - 1) // (3 * 256)) * 256
Q_BLOCK = 128
EPS = 1e-6
NEG = -1e30
IN_SIZES = (HA * 2 * QK_HALF, HA * 2 * QK_HALF, HA * DV_A,
            HB * DK, HB * DK, HB * DV_B, HB * DV_B, HB, HB,
            D_MODEL, D_MODEL)
N_IN = sum(IN_SIZES)

kernel_name = 'hybrid_diffattn_mlstm_gated_decode_step'


def rmsnorm(x, g):
    xf = x.astype(jnp.float32)
    y = xf * lax.rsqrt(jnp.mean(xf * xf, axis=-1, keepdims=True) + EPS)
    return (y * g.astype(jnp.float32)).astype(x.dtype)


def alibi_slopes():
    return jnp.asarray(2.0 ** (-8.0 * np.arange(1, HA + 1) / HA), dtype=jnp.float32)


def diff_attention(q, k, v, q_pos, lam):
    B, Lq = q.shape[:2]
    Lk = k.shape[1]
    k_pos = jnp.arange(Lk, dtype=jnp.int32)
    slopes = alibi_slopes()
    scale = QK_HALF ** -0.5
    k1 = k[..., :QK_HALF]
    k2 = k[..., QK_HALF:]

    def block(qb, pb):
        s1 = jnp.einsum('bqhd,bkhd->bhqk', qb[..., :QK_HALF], k1, preferred_element_type=jnp.float32) * scale
        s2 = jnp.einsum('bqhd,bkhd->bhqk', qb[..., QK_HALF:], k2, preferred_element_type=jnp.float32) * scale
        dist = pb[:, None] - k_pos[None, :]
        bias = -slopes[:, None, None] * dist.astype(jnp.float32)[None]
        mask = (dist >= 0)[None, None]
        p1 = jax.nn.softmax(jnp.where(mask, s1 + bias, NEG), axis=-1)
        p2 = jax.nn.softmax(jnp.where(mask, s2 + bias, NEG), axis=-1)
        a = p1 - lam * p2
        return jnp.einsum('bhqk,bkhd->bqhd', a.astype(v.dtype), v)

    if Lq <= Q_BLOCK:
        return block(q, q_pos)
    nb = -(-Lq // Q_BLOCK)
    pad = nb * Q_BLOCK - Lq
    qp = jnp.pad(q, ((0, 0), (0, pad), (0, 0), (0, 0)))
    pp = jnp.pad(q_pos, (0, pad), mode='edge')
    qs = qp.reshape(B, nb, Q_BLOCK, HA, 2 * QK_HALF).transpose(1, 0, 2, 3, 4)
    ps = pp.reshape(nb, Q_BLOCK)
    out = lax.map(lambda args: block(args[0], args[1]), (qs, ps))
    return out.transpose(1, 0, 2, 3, 4).reshape(B, nb * Q_BLOCK, HA, DV_A)[:, :Lq]


def mlstm_chunkwise(q, k, v, i_pre, f_pre, C0, n0, m0):
    B, L = q.shape[:2]
    f32 = jnp.float32
    c = MLSTM_CHUNK if L >= MLSTM_CHUNK else L
    nc = -(-L // c)
    pad = nc * c - L

    def chunk4(t):
        t = jnp.pad(t.astype(f32), ((0, 0), (0, pad), (0, 0), (0, 0)))
        return t.reshape(B, nc, c, t.shape[2], t.shape[3]).transpose(1, 0, 3, 2, 4)

    def chunk3(t, fill):
        t = jnp.pad(t.astype(f32), ((0, 0), (0, pad), (0, 0)), constant_values=fill)
        return t.reshape(B, nc, c, t.shape[2]).transpose(1, 0, 3, 2)

    qs = chunk4(q.astype(f32) * DK ** -0.5)
    ks = chunk4(k)
    vs = chunk4(v)
    ig = chunk3(i_pre, NEG)
    lf = chunk3(jax.nn.log_sigmoid(f_pre.astype(f32)), 0.0)
    causal = jnp.tril(jnp.ones((c, c), dtype=bool))

    def step(carry, inp):
        C, n, m = carry
        qc, kc, vc, ic, fc = inp
        b = jnp.cumsum(fc, axis=-1)
        g = b + m[..., None]
        D = b[..., :, None] - b[..., None, :] + ic[..., None, :]
        D = jnp.where(causal, D, NEG)
        m_t = jnp.maximum(g, jnp.max(D, axis=-1))
        w_inter = jnp.exp(g - m_t)
        W = jnp.exp(D - m_t[..., None])
        S = jnp.einsum('bhtd,bhsd->bhts', qc, kc) * W
        num = w_inter[..., None] * jnp.einsum('bhtk,bhkv->bhtv', qc, C) + jnp.einsum('bhts,bhsv->bhtv', S, vc)
        den = w_inter * jnp.einsum('bhtk,bhk->bht', qc, n) + jnp.sum(S, axis=-1)
        h = num / jnp.maximum(jnp.abs(den), jnp.exp(-m_t))[..., None]
        wc = W[..., -1, :]
        C_new = w_inter[..., -1, None, None] * C + jnp.einsum('bhs,bhsk,bhsv->bhkv', wc, kc, vc)
        n_new = w_inter[..., -1, None] * n + jnp.einsum('bhs,bhsk->bhk', wc, kc)
        return (C_new, n_new, m_t[..., -1]), h

    (C, n, m), h = lax.scan(step, (C0.astype(f32), n0.astype(f32), m0.astype(f32)), (qs, ks, vs, ig, lf))
    h = h.transpose(1, 0, 3, 2, 4).reshape(B, nc * c, HB, DV_B)[:, :L]
    return h, C, n, m


def layer_forward(x, l, q_pos, past_k, past_v, C0, n0, m0, p):
    B, L, _ = x.shape
    f32 = jnp.float32
    h = rmsnorm(x, p['norm1_g'][l])
    z = jnp.einsum('bld,de->ble', h, p['w_in'][l])
    offsets = np.cumsum(IN_SIZES)[:-1].tolist()
    qa, ka, va, qb, kb, vb, ob, ig, fg, ga, gb = jnp.split(z, offsets, axis=-1)

    qa = qa.reshape(B, L, HA, 2 * QK_HALF)
    ka = ka.reshape(B, L, HA, 2 * QK_HALF)
    va = va.reshape(B, L, HA, DV_A)
    keys = ka if past_k is None else jnp.concatenate([past_k.astype(ka.dtype), ka], axis=1)
    vals = va if past_v is None else jnp.concatenate([past_v.astype(va.dtype), va], axis=1)
    lam_init = 0.8 - 0.6 * math.exp(-0.3 * l)
    lam = (jnp.exp(jnp.sum(p['lam_q1'][l].astype(f32) * p['lam_k1'][l].astype(f32)))
           - jnp.exp(jnp.sum(p['lam_q2'][l].astype(f32) * p['lam_k2'][l].astype(f32))) + lam_init)
    a = diff_attention(qa, keys, vals, q_pos, lam)
    a = (rmsnorm(a, p['subln_g'][l]) * (1.0 - lam_init)).reshape(B, L, WA)

    qb = qb.reshape(B, L, HB, DK)
    kb = kb.reshape(B, L, HB, DK)
    vb = vb.reshape(B, L, HB, DV_B)
    i_pre = ig + p['b_igate'][l]
    f_pre = fg + p['b_fgate'][l]
    hb, C, n, m = mlstm_chunkwise(qb, kb, vb, i_pre, f_pre, C0, n0, m0)
    hb = rmsnorm(hb.astype(x.dtype), p['mlstm_norm_g'][l]) * jax.nn.sigmoid(ob).reshape(B, L, HB, DV_B)
    hb = hb.reshape(B, L, WB)

    merged = jax.nn.sigmoid(ga) * (a @ p['w_proj_a'][l]) + jax.nn.sigmoid(gb) * (hb @ p['w_proj_b'][l])
    x = x + merged @ p['w_out'][l]

    h2 = rmsnorm(x, p['norm2_g'][l])
    gate, up = jnp.split(h2 @ p['w_gu'][l], 2, axis=-1)
    x = x + (jax.nn.silu(gate) * up) @ p['w_down'][l]
    return x, ka, va, C, n, m


def setup_inputs(seed: int = 0) -> dict:
    key = jax.random.key(seed)
    ks = jax.random.split(key, 26)
    f32 = jnp.float32
    n_pages = PAST_LEN // PAGE_SIZE
    n_pool = (DEC_BATCH * n_pages * 5) // 4

    def nrm(k, shape, s):
        return s * jax.random.normal(k, shape, f32)

    perm = jax.random.permutation(ks[7], n_pool)
    page_table = perm[:DEC_BATCH * n_pages].reshape(DEC_BATCH, n_pages).astype(jnp.int32)
    return {
        'x_prompt': nrm(ks[0], (BATCH, SEQ, D_MODEL), 1.0),
        'x_sample': nrm(ks[1], (DEC_BATCH, DEC_SEQ, D_MODEL), 1.0),
        'cache_k': nrm(ks[2], (DEPTH, n_pool, PAGE_SIZE, HA, 2 * QK_HALF), 1.0),
        'cache_v': nrm(ks[3], (DEPTH, n_pool, PAGE_SIZE, HA, DV_A), 1.0),
        'state_C': nrm(ks[4], (DEPTH, DEC_BATCH, HB, DK, DV_B), DK ** -0.5),
        'state_n': nrm(ks[5], (DEPTH, DEC_BATCH, HB, DK), 1.0),
        'state_m': nrm(ks[6], (DEPTH, DEC_BATCH, HB), 1.0),
        'page_table': page_table,
        'norm1_g': 1.0 + nrm(ks[8], (DEPTH, D_MODEL), 0.02),
        'w_in': nrm(ks[9], (DEPTH, D_MODEL, N_IN), D_MODEL ** -0.5),
        'b_igate': nrm(ks[10], (DEPTH, HB), 0.1),
        'b_fgate': 3.0 + nrm(ks[11], (DEPTH, HB), 0.5),
        'lam_q1': nrm(ks[12], (DEPTH, QK_HALF), 0.1),
        'lam_k1': nrm(ks[13], (DEPTH, QK_HALF), 0.1),
        'lam_q2': nrm(ks[14], (DEPTH, QK_HALF), 0.1),
        'lam_k2': nrm(ks[15], (DEPTH, QK_HALF), 0.1),
        'subln_g': 1.0 + nrm(ks[16], (DEPTH, DV_A), 0.02),
        'mlstm_norm_g': 1.0 + nrm(ks[17], (DEPTH, DV_B), 0.02),
        'w_proj_a': nrm(ks[18], (DEPTH, WA, D_MODEL), WA ** -0.5),
        'w_proj_b': nrm(ks[19], (DEPTH, WB, D_MODEL), WB ** -0.5),
        'w_out': nrm(ks[20], (DEPTH, D_MODEL, D_MODEL), D_MODEL ** -0.5),
        'norm2_g': 1.0 + nrm(ks[21], (DEPTH, D_MODEL), 0.02),
        'w_gu': nrm(ks[22], (DEPTH, D_MODEL, 2 * D_FF), D_MODEL ** -0.5),
        'w_down': nrm(ks[23], (DEPTH, D_FF, D_MODEL), D_FF ** -0.5),
        'final_g': 1.0 + nrm(ks[24], (D_MODEL,), 0.02),
    }


def reference(x_prompt, x_sample, cache_k, cache_v, state_C, state_n, state_m, page_table,
              norm1_g, w_in, b_igate, b_fgate, lam_q1, lam_k1, lam_q2, lam_k2, subln_g, mlstm_norm_g,
              w_proj_a, w_proj_b, w_out, norm2_g, w_gu, w_down, final_g):
    p = dict(norm1_g=norm1_g, w_in=w_in, b_igate=b_igate, b_fgate=b_fgate,
             lam_q1=lam_q1, lam_k1=lam_k1, lam_q2=lam_q2, lam_k2=lam_k2,
             subln_g=subln_g, mlstm_norm_g=mlstm_norm_g, w_proj_a=w_proj_a, w_proj_b=w_proj_b,
             w_out=w_out, norm2_g=norm2_g, w_gu=w_gu, w_down=w_down)
    n_pages = page_table.shape[1]
    page = cache_k.shape[2]
    past_len = n_pages * page
    bp, lp = x_prompt.shape[:2]
    bs, ls = x_sample.shape[:2]
    pos_p = jnp.arange(lp, dtype=jnp.int32)
    pos_s = past_len + jnp.arange(ls, dtype=jnp.int32)
    zero_C = jnp.zeros((bp, HB, DK, DV_B), jnp.float32)
    zero_n = jnp.zeros((bp, HB, DK), jnp.float32)
    zero_m = jnp.zeros((bp, HB), jnp.float32)

    xp = x_prompt
    xs = x_sample
    kp_l, vp_l, Cp_l, np_l, mp_l = [], [], [], [], []
    ks_l, vs_l, Cs_l, ns_l, ms_l = [], [], [], [], []
    for l in range(DEPTH):
        xp, k1, v1, C1, n1, m1 = layer_forward(xp, l, pos_p, None, None, zero_C, zero_n, zero_m, p)
        past_k = cache_k[l][page_table].reshape(bs, past_len, HA, 2 * QK_HALF)
        past_v = cache_v[l][page_table].reshape(bs, past_len, HA, DV_A)
        xs, k2, v2, C2, n2, m2 = layer_forward(xs, l, pos_s, past_k, past_v,
                                               state_C[l], state_n[l], state_m[l], p)
        kp_l.append(k1); vp_l.append(v1); Cp_l.append(C1); np_l.append(n1); mp_l.append(m1)
        ks_l.append(k2); vs_l.append(v2); Cs_l.append(C2); ns_l.append(n2); ms_l.append(m2)

    y_prompt = rmsnorm(xp, final_g)
    y_sample = rmsnorm(xs, final_g)
    return (y_prompt, y_sample,
            jnp.stack(kp_l), jnp.stack(vp_l), jnp.stack(Cp_l), jnp.stack(np_l), jnp.stack(mp_l),
            jnp.stack(ks_l), jnp.stack(vs_l), jnp.stack(Cs_l), jnp.stack(ns_l), jnp.stack(ms_l))
```

```python
import functools
import math

import numpy as np
import jax
import jax.numpy as jnp
from jax import lax
from jax.experimental import pallas as pl
from jax.experimental.pallas import tpu as pltpu

F32 = jnp.float32
BF16 = jnp.bfloat16

D_MODEL = 2048
DEPTH = 4
PAGE_SIZE = 128
WA = D_MODEL // 2
DV_A = 128
HA = WA // DV_A
QK_HALF = DV_A // 2
WB = D_MODEL // 2
HB = 4
DK = WB // HB
DV_B = WB // HB
D_FF = 5632
EPS = 1e-6
NEG = -1e30

Z_COLS = 2 * D_MODEL + 3 * WA + 4 * WB
COL_QA = 2 * D_MODEL
COL_KA = COL_QA + WA
COL_VA = COL_KA + WA
COL_QB = COL_VA + WA
COL_KB = COL_QB + WB
COL_VB = COL_KB + WB
COL_OB = COL_VB + WB
GATE_COLS = 128

VMEM_LIMIT = 56 * 1024 * 1024


def _cparams(n_axes, vmem=VMEM_LIMIT):
    return pltpu.CompilerParams(dimension_semantics=("arbitrary",) * n_axes, vmem_limit_bytes=vmem)


def _rms(x, g):
    return x * lax.rsqrt(jnp.mean(x * x, axis=-1, keepdims=True) + EPS) * g


def _rmsnorm_kernel(x_ref, g_ref, o_ref):
    o_ref[...] = _rms(x_ref[...], g_ref[...])


def rmsnorm_rows(x, g, tm):
    m, d = x.shape
    return pl.pallas_call(
        _rmsnorm_kernel,
        grid=(m // tm,),
        in_specs=[pl.BlockSpec((tm, d), lambda i: (i, 0)), pl.BlockSpec((1, d), lambda i: (0, 0))],
        out_specs=pl.BlockSpec((tm, d), lambda i: (i, 0)),
        out_shape=jax.ShapeDtypeStruct((m, d), F32),
        compiler_params=_cparams(1),
        name="final_rmsnorm",
    )(x, g.reshape(1, d))


def _inproj_kernel(x_ref, g_ref, w_ref, wg_ref, z_ref, zg_ref, h_ref):
    @pl.when(pl.program_id(1) == 0)
    def _():
        h = _rms(x_ref[...], g_ref[...]).astype(BF16)
        h_ref[...] = h
        zg_ref[...] = jnp.dot(h, wg_ref[...], preferred_element_type=F32)

    z_ref[...] = jnp.dot(h_ref[...], w_ref[...], preferred_element_type=F32)


def in_proj(x, g, w_main, w_gate, layer, tm, tn):
    m, d = x.shape
    return pl.pallas_call(
        _inproj_kernel,
        grid=(m // tm, Z_COLS // tn),
        in_specs=[
            pl.BlockSpec((tm, d), lambda i, j: (i, 0)),
            pl.BlockSpec((None, 1, d), lambda i, j: (layer, 0, 0)),
            pl.BlockSpec((None, d, tn), lambda i, j: (layer, 0, j)),
            pl.BlockSpec((None, d, GATE_COLS), lambda i, j: (layer, 0, 0)),
        ],
        out_specs=[
            pl.BlockSpec((tm, tn), lambda i, j: (i, j)),
            pl.BlockSpec((tm, GATE_COLS), lambda i, j: (i, 0)),
        ],
        out_shape=[jax.ShapeDtypeStruct((m, Z_COLS), F32), jax.ShapeDtypeStruct((m, GATE_COLS), F32)],
        scratch_shapes=[pltpu.VMEM((tm, d), BF16)],
        compiler_params=_cparams(2),
        name="in_proj",
    )(x, g, w_main, w_gate)


def _attn_prompt_kernel(slope_ref, lam_ref, q_ref, k_ref, v_ref, g_ref, o_ref, kb_ref, vb_ref, *, tq, out_scale):
    h = pl.program_id(1)
    qi = pl.program_id(2)

    @pl.when(qi == 0)
    def _():
        kb_ref[...] = k_ref[...].astype(BF16)
        vb_ref[...] = v_ref[...].astype(BF16)

    slope = slope_ref[h]
    lam = lam_ref[0]
    q = q_ref[...] * (QK_HALF ** -0.5)
    lane = lax.broadcasted_iota(jnp.int32, (tq, DV_A), 1)
    qs = jnp.concatenate([jnp.where(lane < QK_HALF, q, 0.0), jnp.where(lane >= QK_HALF, q, 0.0)],
                         axis=0).astype(BF16)
    row = lax.broadcasted_iota(jnp.int32, (2 * tq, tq), 0)
    col = lax.broadcasted_iota(jnp.int32, (2 * tq, tq), 1)
    rel = col - jnp.where(row >= tq, row - tq, row)
    bias = slope * rel.astype(F32)
    bias_diag = jnp.where(rel <= 0, bias, NEG)

    def step(j, carry, bias_tile):
        m, l, acc = carry
        kj = kb_ref[pl.ds(pl.multiple_of(j * tq, tq), tq), :]
        vj = vb_ref[pl.ds(pl.multiple_of(j * tq, tq), tq), :]
        s = lax.dot_general(qs, kj, (((1,), (1,)), ((), ())), preferred_element_type=F32) + bias_tile
        off = slope * ((qi - j) * tq).astype(F32)
        m_new = jnp.maximum(m, jnp.max(s, axis=-1, keepdims=True) - off)
        alpha = jnp.exp(m - m_new)
        p = jnp.exp(s - (m_new + off))
        l = alpha * l + jnp.sum(p, axis=-1, keepdims=True)
        acc = alpha * acc + jnp.dot(p.astype(BF16), vj, preferred_element_type=F32)
        return m_new, l, acc

    init = (jnp.full((2 * tq, 1), NEG, F32), jnp.zeros((2 * tq, 1), F32), jnp.zeros((2 * tq, DV_A), F32))
    carry = lax.fori_loop(0, qi, lambda j, c: step(j, c, bias), init)
    _, l, acc = step(qi, carry, bias_diag)
    o = acc / l
    a = o[:tq] - lam * o[tq:]
    o_ref[...] = (_rms(a, g_ref[...]) * out_scale).astype(o_ref.dtype)


def attn_prompt(z3, slopes, lam, subln_g, layer, tq):
    b, seq, _ = z3.shape
    lam_init = 0.8 - 0.6 * math.exp(-0.3 * layer)
    kern = functools.partial(_attn_prompt_kernel, tq=tq, out_scale=1.0 - lam_init)
    return pl.pallas_call(
        kern,
        grid=(b, HA, seq // tq),
        in_specs=[
            pl.BlockSpec(memory_space=pltpu.SMEM),
            pl.BlockSpec(memory_space=pltpu.SMEM),
            pl.BlockSpec((None, tq, DV_A), lambda bi, h, qi: (bi, qi, COL_QA // DV_A + h)),
            pl.BlockSpec((None, seq, DV_A), lambda bi, h, qi: (bi, 0, COL_KA // DV_A + h)),
            pl.BlockSpec((None, seq, DV_A), lambda bi, h, qi: (bi, 0, COL_VA // DV_A + h)),
            pl.BlockSpec((None, 1, DV_A), lambda bi, h, qi: (layer, 0, 0)),
        ],
        out_specs=pl.BlockSpec((None, tq, DV_A), lambda bi, h, qi: (bi, qi, h)),
        out_shape=jax.ShapeDtypeStruct((b, seq, WA), BF16),
        scratch_shapes=[pltpu.VMEM((seq, DV_A), BF16), pltpu.VMEM((seq, DV_A), BF16)],
        compiler_params=_cparams(3),
        name="attn_prompt",
    )(slopes, lam, z3, z3, z3, subln_g)


def _attn_decode_kernel(pt_ref, lam_ref, qm_ref, kown_ref, vown_ref, bias_ref, slope_ref, g_ref, *rest,
                        pages_per_step, past_len, out_scale):
    del pt_ref
    k_refs = rest[:pages_per_step]
    v_refs = rest[pages_per_step:2 * pages_per_step]
    o_ref, m_ref, l_ref, acc_ref = rest[2 * pages_per_step:]
    i = pl.program_id(1)
    qm = qm_ref[...]

    @pl.when(i == 0)
    def _():
        ko = kown_ref[...].astype(BF16).astype(F32)
        m_ref[...] = jnp.sum(qm.astype(F32) * ko, axis=-1, keepdims=True)
        l_ref[...] = jnp.ones_like(l_ref)
        acc_ref[...] = vown_ref[...].astype(BF16).astype(F32)

    ss = []
    for gi in range(pages_per_step):
        kp = k_refs[gi][...].astype(BF16)
        s = lax.dot_general(qm, kp, (((1,), (1,)), ((), ())), preferred_element_type=F32)
        page_pos = i * pages_per_step + gi
        off = (page_pos * PAGE_SIZE - past_len).astype(F32)
        ss.append(s + (bias_ref[...] + slope_ref[...] * off))
    s = jnp.concatenate(ss, axis=-1)
    m_old = m_ref[...]
    m_new = jnp.maximum(m_old, jnp.max(s, axis=-1, keepdims=True))
    alpha = jnp.exp(m_old - m_new)
    p = jnp.exp(s - m_new)
    l_ref[...] = alpha * l_ref[...] + jnp.sum(p, axis=-1, keepdims=True)
    pv = jnp.zeros(acc_ref.shape, F32)
    n_rows = PAGE_SIZE * HA
    for gi in range(pages_per_step):
        vp = v_refs[gi][...].astype(BF16)
        pv = pv + jnp.dot(p[:, gi * n_rows:(gi + 1) * n_rows].astype(BF16), vp, preferred_element_type=F32)
    acc_ref[...] = alpha * acc_ref[...] + pv
    m_ref[...] = m_new

    @pl.when(i == pl.num_programs(1) - 1)
    def _():
        o = acc_ref[...] / l_ref[...]
        a = o[:HA] - lam_ref[0] * o[HA:]
        o_ref[...] = (_rms(a, g_ref[...]) * out_scale).astype(o_ref.dtype)


def attn_decode(qm, kown, vown, bias_tile, slope_col, lam, subln_g, cache_k4, cache_v4, page_table_flat, layer,
                pages_per_step):
    bs = qm.shape[0]
    n_pages = page_table_flat.shape[0] // bs
    n_rows = PAGE_SIZE * HA
    lam_init = 0.8 - 0.6 * math.exp(-0.3 * layer)
    kern = functools.partial(_attn_decode_kernel, pages_per_step=pages_per_step,
                             past_len=n_pages * PAGE_SIZE, out_scale=1.0 - lam_init)

    def page_spec(gi):
        return pl.BlockSpec((None, None, n_rows, DV_A),
                            lambda b, i, pt: (layer, pt[b * n_pages + i * pages_per_step + gi], 0, 0))

    grid_spec = pltpu.PrefetchScalarGridSpec(
        num_scalar_prefetch=1,
        grid=(bs, n_pages // pages_per_step),
        in_specs=[
            pl.BlockSpec(memory_space=pltpu.SMEM),
            pl.BlockSpec((None, 2 * HA, DV_A), lambda b, i, pt: (b, 0, 0)),
            pl.BlockSpec((None, 2 * HA, DV_A), lambda b, i, pt: (b, 0, 0)),
            pl.BlockSpec((None, 2 * HA, DV_A), lambda b, i, pt: (b, 0, 0)),
            pl.BlockSpec((2 * HA, n_rows), lambda b, i, pt: (0, 0)),
            pl.BlockSpec((2 * HA, 1), lambda b, i, pt: (0, 0)),
            pl.BlockSpec((None, 1, DV_A), lambda b, i, pt: (layer, 0, 0)),
        ] + [page_spec(gi) for gi in range(pages_per_step)] * 2,
        out_specs=pl.BlockSpec((None, HA, DV_A), lambda b, i, pt: (b, 0, 0)),
        scratch_shapes=[pltpu.VMEM((2 * HA, 1), F32), pltpu.VMEM((2 * HA, 1), F32),
                        pltpu.VMEM((2 * HA, DV_A), F32)],
    )
    return pl.pallas_call(
        kern,
        grid_spec=grid_spec,
        out_shape=jax.ShapeDtypeStruct((bs, HA, DV_A), BF16),
        compiler_params=_cparams(2),
        name="attn_decode",
    )(page_table_flat, lam, qm, kown, vown, bias_tile, slope_col, subln_g,
      *([cache_k4] * pages_per_step), *([cache_v4] * pages_per_step))


def _log_sigmoid(x):
    return jnp.minimum(x, 0.0) - jnp.log1p(jnp.exp(-jnp.abs(x)))


def _mlstm_prompt_kernel(bi_ref, bf_ref, q_ref, k_ref, v_ref, ob_ref, zg_ref, g_ref,
                         hb_ref, c_out_ref, n_out_ref, m_out_ref, c_ref, n_ref, m_ref, *, chunk):
    ci = pl.program_id(1)

    @pl.when(ci == 0)
    def _():
        c_ref[...] = jnp.zeros_like(c_ref)
        n_ref[...] = jnp.zeros_like(n_ref)
        m_ref[...] = jnp.zeros_like(m_ref)

    gates = zg_ref[...]
    gates_t = gates.T
    row = lax.broadcasted_iota(jnp.int32, (chunk, chunk), 0)
    col = lax.broadcasted_iota(jnp.int32, (chunk, chunk), 1)
    causal = col <= row
    g_norm = g_ref[...]
    for h in range(HB):
        sl = slice(h * DK, (h + 1) * DK)
        i_col = gates[:, h:h + 1] + bi_ref[h]
        i_row = gates_t[h:h + 1, :] + bi_ref[h]
        lf_col = _log_sigmoid(gates[:, HB + h:HB + h + 1] + bf_ref[h])
        lf_row = _log_sigmoid(gates_t[HB + h:HB + h + 1, :] + bf_ref[h])
        b_col = jnp.sum(jnp.where(causal, lf_row, 0.0), axis=1, keepdims=True)
        b_row = jnp.sum(jnp.where(row <= col, lf_col, 0.0), axis=0, keepdims=True)
        m_prev = m_ref[h:h + 1, 0:1]
        g_col = b_col + m_prev
        dmat = jnp.where(causal, b_col - b_row + i_row, NEG)
        m_t = jnp.maximum(g_col, jnp.max(dmat, axis=1, keepdims=True))
        w_inter = jnp.exp(g_col - m_t)
        wmat = jnp.exp(dmat - m_t)
        qs = (q_ref[:, sl] * (DK ** -0.5)).astype(BF16)
        kf = k_ref[:, sl]
        kb = kf.astype(BF16)
        vb = v_ref[:, sl].astype(BF16)
        c_old = c_ref[h]
        n_old = n_ref[h:h + 1, :]
        s = lax.dot_general(qs, kb, (((1,), (1,)), ((), ())), preferred_element_type=F32) * wmat
        num = (w_inter * jnp.dot(qs, c_old.astype(BF16), preferred_element_type=F32)
               + jnp.dot(s.astype(BF16), vb, preferred_element_type=F32))
        qn = jnp.sum(qs.astype(F32) * n_old.astype(BF16).astype(F32), axis=-1, keepdims=True)
        den = w_inter * qn + jnp.sum(s, axis=-1, keepdims=True)
        hv = num / jnp.maximum(jnp.abs(den), jnp.exp(-m_t))
        hb = _rms(hv, g_norm) * jax.nn.sigmoid(ob_ref[:, sl])
        hb_ref[:, sl] = hb.astype(hb_ref.dtype)
        m_last = m_t[chunk - 1:chunk, :]
        w_last = w_inter[chunk - 1:chunk, :]
        wc_col = jnp.exp(b_col[chunk - 1:chunk, :] - b_col + i_col - m_last)
        kw = kb.astype(F32) * wc_col.astype(BF16).astype(F32)
        c_new = w_last * c_old + jnp.dot(kw.T.astype(BF16), vb, preferred_element_type=F32)
        n_new = w_last * n_old + jnp.sum(kw, axis=0, keepdims=True)
        c_ref[h] = c_new
        n_ref[h:h + 1, :] = n_new
        m_ref[h:h + 1, :] = jnp.broadcast_to(m_last, (1, m_ref.shape[1]))

    @pl.when(ci == pl.num_programs(1) - 1)
    def _():
        c_out_ref[...] = c_ref[...]
        n_out_ref[...] = n_ref[...]
        m_out_ref[...] = m_ref[...]


def mlstm_prompt(z3, zg3, b_i, b_f, norm_g, layer, chunk):
    b, seq, _ = z3.shape

    def zspec(col):
        return pl.BlockSpec((None, chunk, WB), lambda bi, ci: (bi, ci, col // WB))

    return pl.pallas_call(
        functools.partial(_mlstm_prompt_kernel, chunk=chunk),
        grid=(b, seq // chunk),
        in_specs=[
            pl.BlockSpec(memory_space=pltpu.SMEM),
            pl.BlockSpec(memory_space=pltpu.SMEM),
            zspec(COL_QB), zspec(COL_KB), zspec(COL_VB), zspec(COL_OB),
            pl.BlockSpec((None, chunk, GATE_COLS), lambda bi, ci: (bi, ci, 0)),
            pl.BlockSpec((None, 1, DV_B), lambda bi, ci: (layer, 0, 0)),
        ],
        out_specs=[
            pl.BlockSpec((None, chunk, WB), lambda bi, ci: (bi, ci, 0)),
            pl.BlockSpec((None, HB, DK, DV_B), lambda bi, ci: (bi, 0, 0, 0)),
            pl.BlockSpec((None, HB, DK), lambda bi, ci: (bi, 0, 0)),
            pl.BlockSpec((None, HB, 128), lambda bi, ci: (bi, 0, 0)),
        ],
        out_shape=[
            jax.ShapeDtypeStruct((b, seq, WB), BF16),
            jax.ShapeDtypeStruct((b, HB, DK, DV_B), F32),
            jax.ShapeDtypeStruct((b, HB, DK), F32),
            jax.ShapeDtypeStruct((b, HB, 128), F32),
        ],
        scratch_shapes=[pltpu.VMEM((HB, DK, DV_B), F32), pltpu.VMEM((HB, DK), F32), pltpu.VMEM((HB, 128), F32)],
        compiler_params=_cparams(2),
        name="mlstm_prompt",
    )(b_i, b_f, z3, z3, z3, z3, zg3, norm_g)


def _mlstm_step_kernel(bi_ref, bf_ref, q_ref, kcol_ref, krow_ref, v_ref, ob_ref, zg_ref, g_ref,
                       c0_ref, n0_ref, m0_ref, hb_ref, c_out_ref, n_out_ref, m_out_ref):
    gates = zg_ref[...]
    g_norm = g_ref[...]
    for h in range(HB):
        sl = slice(h * DK, (h + 1) * DK)
        i_pre = gates[:, h:h + 1] + bi_ref[h]
        lf = _log_sigmoid(gates[:, HB + h:HB + h + 1] + bf_ref[h])
        m0 = m0_ref[h:h + 1, 0:1]
        g_st = lf + m0
        m_t = jnp.maximum(g_st, i_pre)
        w_inter = jnp.exp(g_st - m_t)
        w_in = jnp.exp(i_pre - m_t)
        qs = (q_ref[:, sl] * (DK ** -0.5)).astype(BF16)
        k_row = krow_ref[:, sl].astype(BF16).astype(F32)
        k_col = kcol_ref[h].astype(BF16).astype(F32)
        v_row = v_ref[:, sl].astype(BF16).astype(F32)
        c_old = c0_ref[h]
        n_old = n0_ref[h:h + 1, :]
        qf = qs.astype(F32)
        s = jnp.sum(qf * k_row, axis=-1, keepdims=True) * w_in
        q16 = jnp.broadcast_to(qs, (16, DK))
        qc = jnp.dot(q16, c_old.astype(BF16), preferred_element_type=F32)[0:1, :]
        num = w_inter * qc + s.astype(BF16).astype(F32) * v_row
        qn = jnp.sum(qf * n_old.astype(BF16).astype(F32), axis=-1, keepdims=True)
        den = w_inter * qn + s
        hv = num / jnp.maximum(jnp.abs(den), jnp.exp(-m_t))
        hb = _rms(hv, g_norm) * jax.nn.sigmoid(ob_ref[:, sl])
        hb_ref[:, sl] = hb.astype(hb_ref.dtype)
        wb = w_in.astype(BF16).astype(F32)
        kw_col = (k_col * wb).astype(BF16).astype(F32)
        c_out_ref[h] = w_inter * c_old + kw_col * v_row
        n_out_ref[h:h + 1, :] = w_inter * n_old + wb * k_row
        m_out_ref[h:h + 1, :] = jnp.broadcast_to(m_t, (1, m_out_ref.shape[1]))


def mlstm_step(zs3, kcol, zgs3, b_i, b_f, norm_g, state_c, state_n, m0_lanes, layer):
    bs = zs3.shape[0]

    def zspec(col):
        return pl.BlockSpec((None, 1, WB), lambda bi: (bi, 0, col // WB))

    return pl.pallas_call(
        _mlstm_step_kernel,
        grid=(bs,),
        in_specs=[
            pl.BlockSpec(memory_space=pltpu.SMEM),
            pl.BlockSpec(memory_space=pltpu.SMEM),
            zspec(COL_QB),
            pl.BlockSpec((None, HB, DK, 1), lambda bi: (bi, 0, 0, 0)),
            zspec(COL_KB), zspec(COL_VB), zspec(COL_OB),
            pl.BlockSpec((None, 1, GATE_COLS), lambda bi: (bi, 0, 0)),
            pl.BlockSpec((None, 1, DV_B), lambda bi: (layer, 0, 0)),
            pl.BlockSpec((None, None, HB, DK, DV_B), lambda bi: (layer, bi, 0, 0, 0)),
            pl.BlockSpec((None, None, HB, DK), lambda bi: (layer, bi, 0, 0)),
            pl.BlockSpec((None, None, HB, 128), lambda bi: (layer, bi, 0, 0)),
        ],
        out_specs=[
            pl.BlockSpec((None, 1, WB), lambda bi: (bi, 0, 0)),
            pl.BlockSpec((None, HB, DK, DV_B), lambda bi: (bi, 0, 0, 0)),
            pl.BlockSpec((None, HB, DK), lambda bi: (bi, 0, 0)),
            pl.BlockSpec((None, HB, 128), lambda bi: (bi, 0, 0)),
        ],
        out_shape=[
            jax.ShapeDtypeStruct((bs, 1, WB), BF16),
            jax.ShapeDtypeStruct((bs, HB, DK, DV_B), F32),
            jax.ShapeDtypeStruct((bs, HB, DK), F32),
            jax.ShapeDtypeStruct((bs, HB, 128), F32),
        ],
        compiler_params=_cparams(1),
        name="mlstm_step",
    )(b_i, b_f, zs3, kcol, zs3, zs3, zs3, zgs3, norm_g, state_c, state_n, m0_lanes)


def _merge_out_kernel(a_ref, hb_ref, ga_ref, gb_ref, x_ref, wa_ref, wb_ref, wo_ref, o_ref):
    pa = jnp.dot(a_ref[...], wa_ref[...], preferred_element_type=F32)
    pb = jnp.dot(hb_ref[...], wb_ref[...], preferred_element_type=F32)
    merged = jax.nn.sigmoid(ga_ref[...]) * pa + jax.nn.sigmoid(gb_ref[...]) * pb
    o_ref[...] = x_ref[...] + jnp.dot(merged.astype(BF16), wo_ref[...], preferred_element_type=F32)


def merge_out(a, hb, z, x, w_a, w_b, w_o, layer, tm):
    m, d = x.shape
    once = pl.Buffered(1)
    return pl.pallas_call(
        _merge_out_kernel,
        grid=(m // tm,),
        in_specs=[
            pl.BlockSpec((tm, WA), lambda i: (i, 0)),
            pl.BlockSpec((tm, WB), lambda i: (i, 0)),
            pl.BlockSpec((tm, d), lambda i: (i, 0)),
            pl.BlockSpec((tm, d), lambda i: (i, 1)),
            pl.BlockSpec((tm, d), lambda i: (i, 0)),
            pl.BlockSpec((None, WA, d), lambda i: (layer, 0, 0), pipeline_mode=once),
            pl.BlockSpec((None, WB, d), lambda i: (layer, 0, 0), pipeline_mode=once),
            pl.BlockSpec((None, d, d), lambda i: (layer, 0, 0), pipeline_mode=once),
        ],
        out_specs=pl.BlockSpec((tm, d), lambda i: (i, 0)),
        out_shape=jax.ShapeDtypeStruct((m, d), F32),
        compiler_params=_cparams(1),
        name="merge_out",
    )(a, hb, z, z, x, w_a, w_b, w_o)


def _ffn_kernel(x_ref, g_ref, wg_ref, wu_ref, wd_ref, o_ref, h_ref, acc_ref):
    f = pl.program_id(1)

    @pl.when(f == 0)
    def _():
        h_ref[...] = _rms(x_ref[...], g_ref[...]).astype(BF16)
        acc_ref[...] = jnp.zeros_like(acc_ref)

    h = h_ref[...]
    gate = jnp.dot(h, wg_ref[...], preferred_element_type=F32)
    up = jnp.dot(h, wu_ref[...], preferred_element_type=F32)
    act = (gate * jax.nn.sigmoid(gate) * up).astype(BF16)
    acc_ref[...] += jnp.dot(act, wd_ref[...], preferred_element_type=F32)

    @pl.when(f == pl.num_programs(1) - 1)
    def _():
        o_ref[...] = x_ref[...] + acc_ref[...]


def ffn(x, g, w_gu, w_d, layer, tm, tf):
    m, d = x.shape
    nf = D_FF // tf
    return pl.pallas_call(
        _ffn_kernel,
        grid=(m // tm, nf),
        in_specs=[
            pl.BlockSpec((tm, d), lambda i, f: (i, 0)),
            pl.BlockSpec((None, 1, d), lambda i, f: (layer, 0, 0)),
            pl.BlockSpec((None, d, tf), lambda i, f: (layer, 0, f)),
            pl.BlockSpec((None, d, tf), lambda i, f: (layer, 0, nf + f)),
            pl.BlockSpec((None, tf, d), lambda i, f: (layer, f, 0)),
        ],
        out_specs=pl.BlockSpec((tm, d), lambda i, f: (i, 0)),
        out_shape=jax.ShapeDtypeStruct((m, d), F32),
        scratch_shapes=[pltpu.VMEM((tm, d), BF16), pltpu.VMEM((tm, d), F32)],
        compiler_params=_cparams(2),
        name="ffn",
    )(x, g, w_gu, w_gu, w_d)


def kernel(x_prompt, x_sample, cache_k, cache_v, state_C, state_n, state_m, page_table, norm1_g, w_in, b_igate,
           b_fgate, lam_q1, lam_k1, lam_q2, lam_k2, subln_g, mlstm_norm_g, w_proj_a, w_proj_b, w_out, norm2_g,
           w_gu, w_down, final_g):
    bp, lp, d = x_prompt.shape
    bs = x_sample.shape[0]
    n_pool = cache_k.shape[1]
    n_pages = page_table.shape[1]
    past_len = n_pages * PAGE_SIZE
    n_rows = PAGE_SIZE * HA

    gate_lo = 3 * WA + 4 * WB
    w_main = jnp.concatenate([w_in[:, :, gate_lo + 2 * HB:], w_in[:, :, :gate_lo]], axis=-1).astype(BF16)
    w_gate = jnp.pad(w_in[:, :, gate_lo:gate_lo + 2 * HB], ((0, 0), (0, 0), (0, GATE_COLS - 2 * HB))).astype(BF16)
    w_a = w_proj_a.astype(BF16)
    w_b = w_proj_b.astype(BF16)
    w_o = w_out.astype(BF16)
    w_gu_b = w_gu.astype(BF16)
    w_d = w_down.astype(BF16)
    norm1 = norm1_g.reshape(DEPTH, 1, d)
    norm2 = norm2_g.reshape(DEPTH, 1, d)
    subln = subln_g.reshape(DEPTH, 1, DV_A)
    mnorm = mlstm_norm_g.reshape(DEPTH, 1, DV_B)

    slopes = jnp.asarray(2.0 ** (-8.0 * np.arange(1, HA + 1) / HA), dtype=F32)
    lam_all = (jnp.exp(jnp.sum(lam_q1 * lam_k1, axis=-1)) - jnp.exp(jnp.sum(lam_q2 * lam_k2, axis=-1))
               + jnp.asarray([0.8 - 0.6 * math.exp(-0.3 * l) for l in range(DEPTH)], dtype=F32))

    key_head = np.arange(n_rows) % HA
    key_tok = np.arange(n_rows) // HA
    q_head = np.arange(2 * HA) % HA
    slopes_np = 2.0 ** (-8.0 * np.arange(1, HA + 1) / HA)
    bias_np = np.where(key_head[None, :] == q_head[:, None], slopes_np[q_head][:, None] * key_tok[None, :], NEG)
    bias_tile = jnp.asarray(bias_np, dtype=F32)
    slope_col = jnp.asarray(slopes_np[q_head][:, None], dtype=F32)
    half_mask = jnp.asarray((np.arange(DV_A)[None, None, :] // QK_HALF) == np.arange(2)[:, None, None])

    cache_k4 = cache_k.reshape(DEPTH, n_pool, n_rows, DV_A)
    cache_v4 = cache_v.reshape(DEPTH, n_pool, n_rows, DV_A)
    pt_flat = page_table.reshape(-1)
    m0_lanes = jnp.broadcast_to(state_m[..., None], state_m.shape + (128,))

    xp = x_prompt.reshape(bp * lp, d)
    xs = x_sample.reshape(bs, d)
    outs = {k: [] for k in ("kp", "vp", "cp", "np", "mp", "ks", "vs", "cs", "ns", "ms")}
    for l in range(DEPTH):
        lam = lam_all[l:l + 1]
        z, zg = in_proj(xp, norm1, w_main, w_gate, l, tm=1024, tn=1024)
        z3 = z.reshape(bp, lp, Z_COLS)
        outs["kp"].append(z3[:, :, COL_KA:COL_KA + WA].reshape(bp, lp, HA, DV_A))
        outs["vp"].append(z3[:, :, COL_VA:COL_VA + WA].reshape(bp, lp, HA, DV_A))
        a = attn_prompt(z3, slopes, lam, subln, l, tq=256)
        hb, c_p, n_p, m_p = mlstm_prompt(z3, zg.reshape(bp, lp, GATE_COLS), b_igate[l], b_fgate[l], mnorm, l,
                                         chunk=256)
        outs["cp"].append(c_p)
        outs["np"].append(n_p)
        outs["mp"].append(m_p[:, :, 0])
        xp = merge_out(a.reshape(bp * lp, WA), hb.reshape(bp * lp, WB), z, xp, w_a, w_b, w_o, l, tm=256)
        xp = ffn(xp, norm2, w_gu_b, w_d, l, tm=512, tf=512)

        zs, zgs = in_proj(xs, norm1, w_main, w_gate, l, tm=bs, tn=1024)
        k_new = zs[:, COL_KA:COL_KA + WA].reshape(bs, HA, DV_A)
        v_new = zs[:, COL_VA:COL_VA + WA].reshape(bs, HA, DV_A)
        outs["ks"].append(k_new.reshape(bs, 1, HA, DV_A))
        outs["vs"].append(v_new.reshape(bs, 1, HA, DV_A))
        q_new = zs[:, COL_QA:COL_QA + WA].reshape(bs, 1, HA, DV_A) * (QK_HALF ** -0.5)
        qm = jnp.where(half_mask[None], q_new, 0.0).reshape(bs, 2 * HA, DV_A).astype(BF16)
        kown = jnp.tile(k_new, (1, 2, 1))
        vown = jnp.tile(v_new, (1, 2, 1))
        a_s = attn_decode(qm, kown, vown, bias_tile, slope_col, lam, subln, cache_k4, cache_v4, pt_flat, l,
                          pages_per_step=4)
        zs3 = zs.reshape(bs, 1, Z_COLS)
        kcol = zs[:, COL_KB:COL_KB + WB].reshape(bs, HB, DK, 1)
        hb_s, c_s, n_s, m_s = mlstm_step(zs3, kcol, zgs.reshape(bs, 1, GATE_COLS), b_igate[l], b_fgate[l], mnorm,
                                         state_C, state_n, m0_lanes, l)
        outs["cs"].append(c_s)
        outs["ns"].append(n_s)
        outs["ms"].append(m_s[:, :, 0])
        xs = merge_out(a_s.reshape(bs, WA), hb_s.reshape(bs, WB), zs, xs, w_a, w_b, w_o, l, tm=bs)
        xs = ffn(xs, norm2, w_gu_b, w_d, l, tm=bs, tf=512)

    y_prompt = rmsnorm_rows(xp, final_g, tm=512).reshape(bp, lp, d)
    y_sample = rmsnorm_rows(xs, final_g, tm=bs).reshape(bs, 1, d)
    st = {k: jnp.stack(v) for k, v in outs.items()}
    return (y_prompt, y_sample, st["kp"], st["vp"], st["cp"], st["np"], st["mp"],
            st["ks"], st["vs"], st["cs"], st["ns"], st["ms"])
```

```python
import functools
import math

import numpy as np
import jax
import jax.numpy as jnp
from jax import lax
from jax.experimental import pallas as pl
from jax.experimental.pallas import tpu as pltpu

F32 = jnp.float32
BF16 = jnp.bfloat16

D_MODEL = 2048
DEPTH = 4
PAGE_SIZE = 128
WA = D_MODEL // 2
DV_A = 128
HA = WA // DV_A
QK_HALF = DV_A // 2
WB = D_MODEL // 2
HB = 4
DK = WB // HB
DV_B = WB // HB
D_FF = 5632
EPS = 1e-6
NEG = -1e30

Z_COLS = 2 * D_MODEL + 3 * WA + 4 * WB
COL_QA = 2 * D_MODEL
COL_KA = COL_QA + WA
COL_VA = COL_KA + WA
COL_QB = COL_VA + WA
COL_KB = COL_QB + WB
COL_VB = COL_KB + WB
COL_OB = COL_VB + WB
GATE_COLS = 128

VMEM_LIMIT = 56 * 1024 * 1024


def _cparams(n_axes, vmem=VMEM_LIMIT):
    return pltpu.CompilerParams(dimension_semantics=("arbitrary",) * n_axes, vmem_limit_bytes=vmem)


def _rms(x, g):
    return x * lax.rsqrt(jnp.mean(x * x, axis=-1, keepdims=True) + EPS) * g


def _rmsnorm_kernel(x_ref, g_ref, o_ref):
    o_ref[...] = _rms(x_ref[...], g_ref[...])


def rmsnorm_rows(x, g, tm):
    m, d = x.shape
    return pl.pallas_call(
        _rmsnorm_kernel,
        grid=(m // tm,),
        in_specs=[pl.BlockSpec((tm, d), lambda i: (i, 0)), pl.BlockSpec((1, d), lambda i: (0, 0))],
        out_specs=pl.BlockSpec((tm, d), lambda i: (i, 0)),
        out_shape=jax.ShapeDtypeStruct((m, d), F32),
        compiler_params=_cparams(1),
        name="final_rmsnorm",
    )(x, g.reshape(1, d))


def _inproj_kernel(x_ref, g_ref, wt_ref, wh_ref, wg_ref, z_ref, zg_ref, h_ref, *, n_tail):
    j = pl.program_id(1)

    @pl.when(j == 0)
    def _():
        h = _rms(x_ref[...], g_ref[...]).astype(BF16)
        h_ref[...] = h
        zg_ref[...] = jnp.dot(h, wg_ref[...], preferred_element_type=F32)

    @pl.when(j < n_tail)
    def _():
        z_ref[...] = jnp.dot(h_ref[...], wt_ref[...], preferred_element_type=F32)

    @pl.when(j >= n_tail)
    def _():
        z_ref[...] = jnp.dot(h_ref[...], wh_ref[...], preferred_element_type=F32)


def in_proj(x, g, w_tail, w_head, w_gate, layer, tm, tn):
    m, d = x.shape
    n_tail = w_tail.shape[-1] // tn
    return pl.pallas_call(
        functools.partial(_inproj_kernel, n_tail=n_tail),
        grid=(m // tm, Z_COLS // tn),
        in_specs=[
            pl.BlockSpec((tm, d), lambda i, j: (i, 0)),
            pl.BlockSpec((None, 1, d), lambda i, j: (layer, 0, 0)),
            pl.BlockSpec((None, d, tn), lambda i, j: (layer, 0, jnp.minimum(j, n_tail - 1))),
            pl.BlockSpec((None, d, tn), lambda i, j: (layer, 0, jnp.maximum(j - n_tail, 0))),
            pl.BlockSpec((None, d, GATE_COLS), lambda i, j: (layer, 0, 0)),
        ],
        out_specs=[
            pl.BlockSpec((tm, tn), lambda i, j: (i, j)),
            pl.BlockSpec((tm, GATE_COLS), lambda i, j: (i, 0)),
        ],
        out_shape=[jax.ShapeDtypeStruct((m, Z_COLS), F32), jax.ShapeDtypeStruct((m, GATE_COLS), F32)],
        scratch_shapes=[pltpu.VMEM((tm, d), BF16)],
        compiler_params=_cparams(2),
        name="in_proj",
    )(x, g, w_tail, w_head, w_gate)


def _attn_prompt_kernel(slope_ref, lam_ref, q_ref, k_ref, v_ref, g_ref, o_ref, kb_ref, vt_ref, s_ref, m_ref, acc_ref,
                        *, tq, tk, out_scale):
    h = pl.program_id(1)
    qi = pl.program_id(2)
    n_blocks = vt_ref.shape[0]
    n_sub = tq // tk

    @pl.when(qi == 0)
    def _():
        seq = k_ref.shape[0]
        kpos = lax.broadcasted_iota(jnp.int32, (seq, DV_A), 0)
        flane = lax.broadcasted_iota(jnp.int32, (seq, DV_A), 1)
        kfeat = jnp.where(flane == 0, jnp.bitwise_and(kpos, tk - 1).astype(F32), jnp.where(flane <= 2, 1.0, 0.0))
        kb_ref[:, :DV_A] = k_ref[...].astype(BF16)
        kb_ref[:, DV_A:] = kfeat.astype(BF16)
        for jb in range(n_blocks):
            vt_ref[jb, :DV_A, :] = v_ref[jb * tk:(jb + 1) * tk, :].T.astype(BF16)
            vt_ref[jb, DV_A:, :] = jnp.ones((vt_ref.shape[1] - DV_A, tk), BF16)

    slope = slope_ref[h]
    lam = lam_ref[0]
    q = q_ref[...] * (QK_HALF ** -0.5)
    lane = lax.broadcasted_iota(jnp.int32, (tq, DV_A), 1)
    qs = jnp.concatenate([jnp.where(lane < QK_HALF, q, 0.0), jnp.where(lane >= QK_HALF, q, 0.0)], axis=0)
    qpos = jnp.bitwise_and(lax.broadcasted_iota(jnp.int32, (2 * tq, DV_A), 0), tq - 1)
    flane = lax.broadcasted_iota(jnp.int32, (2 * tq, DV_A), 1)
    qfeat = jnp.where(flane == 0, slope,
                      jnp.where(flane == 1, -slope * jnp.bitwise_and(qpos, 255).astype(F32),
                                jnp.where(flane == 2, -slope * (qpos - jnp.bitwise_and(qpos, 255)).astype(F32), 0.0)))
    qaug = jnp.concatenate([qs, qfeat], axis=1).astype(BF16)

    def scores(jb):
        kj = kb_ref[pl.ds(pl.multiple_of(jb * tk, tk), tk), :]
        return lax.dot_general(kj, qaug, (((1,), (1,)), ((), ())), preferred_element_type=F32)

    def softmax_pv(jb, s, off):
        m = m_ref[...]
        m_new = jnp.maximum(m, jnp.max(s, axis=0, keepdims=True) - off)
        alpha = jnp.exp(m - m_new)
        p = jnp.exp(s - (m_new + off))
        acc_ref[...] = alpha * acc_ref[...] + jnp.dot(vt_ref[jb], p.astype(BF16), preferred_element_type=F32)
        m_ref[...] = m_new

    def causal(s, shift):
        krow = lax.broadcasted_iota(jnp.int32, (tk, 2 * tq), 0)
        qcol = jnp.bitwise_and(lax.broadcasted_iota(jnp.int32, (tk, 2 * tq), 1), tq - 1)
        return jnp.where(krow + shift <= qcol, s, NEG)

    def full_blocks(jj, carry):
        for u in range(n_sub):
            jb = jj * n_sub + u
            s_ref[(u + 1) % 2] = scores(jb + 1)
            softmax_pv(jb, s_ref[u % 2], slope * (qi * tq - jb * tk).astype(F32))
        return carry

    s_ref[0] = scores(0)
    m_ref[...] = jnp.full(m_ref.shape, NEG, F32)
    acc_ref[...] = jnp.zeros(acc_ref.shape, F32)
    lax.fori_loop(0, qi, full_blocks, 0)
    for u in range(n_sub):
        jb = qi * n_sub + u
        if u + 1 < n_sub:
            s_ref[(u + 1) % 2] = scores(jb + 1)
        softmax_pv(jb, causal(s_ref[u % 2], u * tk), slope * (-u * tk))
    o = acc_ref[:DV_A, :] / acc_ref[DV_A:DV_A + 1, :]
    a = o[:, :tq] - lam * o[:, tq:]
    y = a * lax.rsqrt(jnp.mean(a * a, axis=0, keepdims=True) + EPS) * g_ref[...] * out_scale
    o_ref[...] = y.T.astype(o_ref.dtype)


def attn_prompt(z3, slopes, lam, subln_col, layer, tq, tk):
    b, seq, _ = z3.shape
    assert tk <= 256 and tq == 2 * tk and tq & (tq - 1) == 0 and tk & (tk - 1) == 0
    lam_init = 0.8 - 0.6 * math.exp(-0.3 * layer)
    kern = functools.partial(_attn_prompt_kernel, tq=tq, tk=tk, out_scale=1.0 - lam_init)
    return pl.pallas_call(
        kern,
        grid=(b, HA, seq // tq),
        in_specs=[
            pl.BlockSpec(memory_space=pltpu.SMEM),
            pl.BlockSpec(memory_space=pltpu.SMEM),
            pl.BlockSpec((None, tq, DV_A), lambda bi, h, qi: (bi, qi, COL_QA // DV_A + h)),
            pl.BlockSpec((None, seq, DV_A), lambda bi, h, qi: (bi, 0, COL_KA // DV_A + h)),
            pl.BlockSpec((None, seq, DV_A), lambda bi, h, qi: (bi, 0, COL_VA // DV_A + h)),
            pl.BlockSpec((None, DV_A, 1), lambda bi, h, qi: (layer, 0, 0)),
        ],
        out_specs=pl.BlockSpec((None, tq, DV_A), lambda bi, h, qi: (bi, qi, h)),
        out_shape=jax.ShapeDtypeStruct((b, seq, WA), BF16),
        scratch_shapes=[pltpu.VMEM((seq, 2 * DV_A), BF16), pltpu.VMEM((seq // tk, DV_A + 16, tk), BF16),
                        pltpu.VMEM((2, tk, 2 * tq), F32), pltpu.VMEM((1, 2 * tq), F32),
                        pltpu.VMEM((DV_A + 16, 2 * tq), F32)],
        compiler_params=_cparams(3),
        name="attn_prompt",
    )(slopes, lam, z3, z3, z3, subln_col)


def _attn_decode_kernel(pt_ref, lam_ref, qm_ref, kown_ref, vown_ref, bias_ref, slope_ref, g_ref, *rest,
                        pages_per_step, past_len, out_scale):
    del pt_ref
    k_refs = rest[:pages_per_step]
    v_refs = rest[pages_per_step:2 * pages_per_step]
    o_ref, m_ref, l_ref, acc_ref = rest[2 * pages_per_step:]
    i = pl.program_id(1)
    qm = qm_ref[...]

    @pl.when(i == 0)
    def _():
        ko = kown_ref[...].astype(BF16).astype(F32)
        m_ref[...] = jnp.sum(qm.astype(F32) * ko, axis=-1, keepdims=True)
        l_ref[...] = jnp.ones_like(l_ref)
        acc_ref[...] = vown_ref[...].astype(BF16).astype(F32)

    ss = []
    for gi in range(pages_per_step):
        kp = k_refs[gi][...].astype(BF16)
        s = lax.dot_general(qm, kp, (((1,), (1,)), ((), ())), preferred_element_type=F32)
        page_pos = i * pages_per_step + gi
        off = (page_pos * PAGE_SIZE - past_len).astype(F32)
        ss.append(s + (bias_ref[...] + slope_ref[...] * off))
    s = jnp.concatenate(ss, axis=-1)
    m_old = m_ref[...]
    m_new = jnp.maximum(m_old, jnp.max(s, axis=-1, keepdims=True))
    alpha = jnp.exp(m_old - m_new)
    p = jnp.exp(s - m_new)
    l_ref[...] = alpha * l_ref[...] + jnp.sum(p, axis=-1, keepdims=True)
    pv = jnp.zeros(acc_ref.shape, F32)
    n_rows = PAGE_SIZE * HA
    for gi in range(pages_per_step):
        vp = v_refs[gi][...].astype(BF16)
        pv = pv + jnp.dot(p[:, gi * n_rows:(gi + 1) * n_rows].astype(BF16), vp, preferred_element_type=F32)
    acc_ref[...] = alpha * acc_ref[...] + pv
    m_ref[...] = m_new

    @pl.when(i == pl.num_programs(1) - 1)
    def _():
        o = acc_ref[...] / l_ref[...]
        a = o[:HA] - lam_ref[0] * o[HA:]
        o_ref[...] = (_rms(a, g_ref[...]) * out_scale).astype(o_ref.dtype)


def attn_decode(qm, kown, vown, bias_tile, slope_col, lam, subln_g, cache_k4, cache_v4, page_table_flat, layer,
                pages_per_step):
    bs = qm.shape[0]
    n_pages = page_table_flat.shape[0] // bs
    n_rows = PAGE_SIZE * HA
    lam_init = 0.8 - 0.6 * math.exp(-0.3 * layer)
    kern = functools.partial(_attn_decode_kernel, pages_per_step=pages_per_step,
                             past_len=n_pages * PAGE_SIZE, out_scale=1.0 - lam_init)

    def page_spec(gi):
        return pl.BlockSpec((None, None, n_rows, DV_A),
                            lambda b, i, pt: (layer, pt[b * n_pages + i * pages_per_step + gi], 0, 0))

    grid_spec = pltpu.PrefetchScalarGridSpec(
        num_scalar_prefetch=1,
        grid=(bs, n_pages // pages_per_step),
        in_specs=[
            pl.BlockSpec(memory_space=pltpu.SMEM),
            pl.BlockSpec((None, 2 * HA, DV_A), lambda b, i, pt: (b, 0, 0)),
            pl.BlockSpec((None, 2 * HA, DV_A), lambda b, i, pt: (b, 0, 0)),
            pl.BlockSpec((None, 2 * HA, DV_A), lambda b, i, pt: (b, 0, 0)),
            pl.BlockSpec((2 * HA, n_rows), lambda b, i, pt: (0, 0)),
            pl.BlockSpec((2 * HA, 1), lambda b, i, pt: (0, 0)),
            pl.BlockSpec((None, 1, DV_A), lambda b, i, pt: (layer, 0, 0)),
        ] + [page_spec(gi) for gi in range(pages_per_step)] * 2,
        out_specs=pl.BlockSpec((None, HA, DV_A), lambda b, i, pt: (b, 0, 0)),
        scratch_shapes=[pltpu.VMEM((2 * HA, 1), F32), pltpu.VMEM((2 * HA, 1), F32),
                        pltpu.VMEM((2 * HA, DV_A), F32)],
    )
    return pl.pallas_call(
        kern,
        grid_spec=grid_spec,
        out_shape=jax.ShapeDtypeStruct((bs, HA, DV_A), BF16),
        compiler_params=_cparams(2),
        name="attn_decode",
    )(page_table_flat, lam, qm, kown, vown, bias_tile, slope_col, subln_g,
      *([cache_k4] * pages_per_step), *([cache_v4] * pages_per_step))


def _log_sigmoid(x):
    return jnp.minimum(x, 0.0) - jnp.log1p(jnp.exp(-jnp.abs(x)))


def _mlstm_prompt_kernel(bi_ref, bf_ref, q_ref, k_ref, v_ref, ob_ref, zg_ref, g_ref,
                         hb_ref, c_out_ref, n_out_ref, m_out_ref, c_ref, n_ref, m_ref, *, chunk):
    ci = pl.program_id(1)

    @pl.when(ci == 0)
    def _():
        c_ref[...] = jnp.zeros_like(c_ref)
        n_ref[...] = jnp.zeros_like(n_ref)
        m_ref[...] = jnp.zeros_like(m_ref)

    gates = zg_ref[...]
    gates_t = gates.T
    row = lax.broadcasted_iota(jnp.int32, (chunk, chunk), 0)
    col = lax.broadcasted_iota(jnp.int32, (chunk, chunk), 1)
    causal = col <= row
    g_norm = g_ref[...]
    for h in range(HB):
        sl = slice(h * DK, (h + 1) * DK)
        i_col = gates[:, h:h + 1] + bi_ref[h]
        i_row = gates_t[h:h + 1, :] + bi_ref[h]
        lf_col = _log_sigmoid(gates[:, HB + h:HB + h + 1] + bf_ref[h])
        lf_row = _log_sigmoid(gates_t[HB + h:HB + h + 1, :] + bf_ref[h])
        b_col = jnp.sum(jnp.where(causal, lf_row, 0.0), axis=1, keepdims=True)
        b_row = jnp.sum(jnp.where(row <= col, lf_col, 0.0), axis=0, keepdims=True)
        m_prev = m_ref[h:h + 1, 0:1]
        g_col = b_col + m_prev
        dmat = jnp.where(causal, b_col - b_row + i_row, NEG)
        m_t = jnp.maximum(g_col, jnp.max(dmat, axis=1, keepdims=True))
        w_inter = jnp.exp(g_col - m_t)
        wmat = jnp.exp(dmat - m_t)
        qs = (q_ref[:, sl] * (DK ** -0.5)).astype(BF16)
        kf = k_ref[:, sl]
        kb = kf.astype(BF16)
        vb = v_ref[:, sl].astype(BF16)
        c_old = c_ref[h]
        n_old = n_ref[h:h + 1, :]
        s = lax.dot_general(qs, kb, (((1,), (1,)), ((), ())), preferred_element_type=F32) * wmat
        num = (w_inter * jnp.dot(qs, c_old.astype(BF16), preferred_element_type=F32)
               + jnp.dot(s.astype(BF16), vb, preferred_element_type=F32))
        qn = jnp.sum(qs.astype(F32) * n_old.astype(BF16).astype(F32), axis=-1, keepdims=True)
        den = w_inter * qn + jnp.sum(s, axis=-1, keepdims=True)
        hv = num / jnp.maximum(jnp.abs(den), jnp.exp(-m_t))
        hb = _rms(hv, g_norm) * jax.nn.sigmoid(ob_ref[:, sl])
        hb_ref[:, sl] = hb.astype(hb_ref.dtype)
        m_last = m_t[chunk - 1:chunk, :]
        w_last = w_inter[chunk - 1:chunk, :]
        wc_col = jnp.exp(b_col[chunk - 1:chunk, :] - b_col + i_col - m_last)
        kw = kb.astype(F32) * wc_col.astype(BF16).astype(F32)
        c_new = w_last * c_old + jnp.dot(kw.T.astype(BF16), vb, preferred_element_type=F32)
        n_new = w_last * n_old + jnp.sum(kw, axis=0, keepdims=True)
        c_ref[h] = c_new
        n_ref[h:h + 1, :] = n_new
        m_ref[h:h + 1, :] = jnp.broadcast_to(m_last, (1, m_ref.shape[1]))

    @pl.when(ci == pl.num_programs(1) - 1)
    def _():
        c_out_ref[...] = c_ref[...]
        n_out_ref[...] = n_ref[...]
        m_out_ref[...] = m_ref[...]


def mlstm_prompt(z3, zg3, b_i, b_f, norm_g, layer, chunk):
    b, seq, _ = z3.shape

    def zspec(col):
        return pl.BlockSpec((None, chunk, WB), lambda bi, ci: (bi, ci, col // WB))

    return pl.pallas_call(
        functools.partial(_mlstm_prompt_kernel, chunk=chunk),
        grid=(b, seq // chunk),
        in_specs=[
            pl.BlockSpec(memory_space=pltpu.SMEM),
            pl.BlockSpec(memory_space=pltpu.SMEM),
            zspec(COL_QB), zspec(COL_KB), zspec(COL_VB), zspec(COL_OB),
            pl.BlockSpec((None, chunk, GATE_COLS), lambda bi, ci: (bi, ci, 0)),
            pl.BlockSpec((None, 1, DV_B), lambda bi, ci: (layer, 0, 0)),
        ],
        out_specs=[
            pl.BlockSpec((None, chunk, WB), lambda bi, ci: (bi, ci, 0)),
            pl.BlockSpec((None, HB, DK, DV_B), lambda bi, ci: (bi, 0, 0, 0)),
            pl.BlockSpec((None, HB, DK), lambda bi, ci: (bi, 0, 0)),
            pl.BlockSpec((None, HB, 128), lambda bi, ci: (bi, 0, 0)),
        ],
        out_shape=[
            jax.ShapeDtypeStruct((b, seq, WB), BF16),
            jax.ShapeDtypeStruct((b, HB, DK, DV_B), F32),
            jax.ShapeDtypeStruct((b, HB, DK), F32),
            jax.ShapeDtypeStruct((b, HB, 128), F32),
        ],
        scratch_shapes=[pltpu.VMEM((HB, DK, DV_B), F32), pltpu.VMEM((HB, DK), F32), pltpu.VMEM((HB, 128), F32)],
        compiler_params=_cparams(2),
        name="mlstm_prompt",
    )(b_i, b_f, z3, z3, z3, z3, zg3, norm_g)


def _mlstm_step_kernel(bi_ref, bf_ref, q_ref, kcol_ref, krow_ref, v_ref, ob_ref, zg_ref, g_ref,
                       c0_ref, n0_ref, m0_ref, hb_ref, c_out_ref, n_out_ref, m_out_ref):
    gates = zg_ref[...]
    g_norm = g_ref[...]
    for h in range(HB):
        sl = slice(h * DK, (h + 1) * DK)
        i_pre = gates[:, h:h + 1] + bi_ref[h]
        lf = _log_sigmoid(gates[:, HB + h:HB + h + 1] + bf_ref[h])
        m0 = m0_ref[h:h + 1, 0:1]
        g_st = lf + m0
        m_t = jnp.maximum(g_st, i_pre)
        w_inter = jnp.exp(g_st - m_t)
        w_in = jnp.exp(i_pre - m_t)
        qs = (q_ref[:, sl] * (DK ** -0.5)).astype(BF16)
        k_row = krow_ref[:, sl].astype(BF16).astype(F32)
        k_col = kcol_ref[h].astype(BF16).astype(F32)
        v_row = v_ref[:, sl].astype(BF16).astype(F32)
        c_old = c0_ref[h]
        n_old = n0_ref[h:h + 1, :]
        qf = qs.astype(F32)
        s = jnp.sum(qf * k_row, axis=-1, keepdims=True) * w_in
        q16 = jnp.broadcast_to(qs, (16, DK))
        qc = jnp.dot(q16, c_old.astype(BF16), preferred_element_type=F32)[0:1, :]
        num = w_inter * qc + s.astype(BF16).astype(F32) * v_row
        qn = jnp.sum(qf * n_old.astype(BF16).astype(F32), axis=-1, keepdims=True)
        den = w_inter * qn + s
        hv = num / jnp.maximum(jnp.abs(den), jnp.exp(-m_t))
        hb = _rms(hv, g_norm) * jax.nn.sigmoid(ob_ref[:, sl])
        hb_ref[:, sl] = hb.astype(hb_ref.dtype)
        wb = w_in.astype(BF16).astype(F32)
        kw_col = (k_col * wb).astype(BF16).astype(F32)
        c_out_ref[h] = w_inter * c_old + kw_col * v_row
        n_out_ref[h:h + 1, :] = w_inter * n_old + wb * k_row
        m_out_ref[h:h + 1, :] = jnp.broadcast_to(m_t, (1, m_out_ref.shape[1]))


def mlstm_step(zs3, kcol, zgs3, b_i, b_f, norm_g, state_c, state_n, m0_lanes, layer):
    bs = zs3.shape[0]

    def zspec(col):
        return pl.BlockSpec((None, 1, WB), lambda bi: (bi, 0, col // WB))

    return pl.pallas_call(
        _mlstm_step_kernel,
        grid=(bs,),
        in_specs=[
            pl.BlockSpec(memory_space=pltpu.SMEM),
            pl.BlockSpec(memory_space=pltpu.SMEM),
            zspec(COL_QB),
            pl.BlockSpec((None, HB, DK, 1), lambda bi: (bi, 0, 0, 0)),
            zspec(COL_KB), zspec(COL_VB), zspec(COL_OB),
            pl.BlockSpec((None, 1, GATE_COLS), lambda bi: (bi, 0, 0)),
            pl.BlockSpec((None, 1, DV_B), lambda bi: (layer, 0, 0)),
            pl.BlockSpec((None, None, HB, DK, DV_B), lambda bi: (layer, bi, 0, 0, 0)),
            pl.BlockSpec((None, None, HB, DK), lambda bi: (layer, bi, 0, 0)),
            pl.BlockSpec((None, None, HB, 128), lambda bi: (layer, bi, 0, 0)),
        ],
        out_specs=[
            pl.BlockSpec((None, 1, WB), lambda bi: (bi, 0, 0)),
            pl.BlockSpec((None, HB, DK, DV_B), lambda bi: (bi, 0, 0, 0)),
            pl.BlockSpec((None, HB, DK), lambda bi: (bi, 0, 0)),
            pl.BlockSpec((None, HB, 128), lambda bi: (bi, 0, 0)),
        ],
        out_shape=[
            jax.ShapeDtypeStruct((bs, 1, WB), BF16),
            jax.ShapeDtypeStruct((bs, HB, DK, DV_B), F32),
            jax.ShapeDtypeStruct((bs, HB, DK), F32),
            jax.ShapeDtypeStruct((bs, HB, 128), F32),
        ],
        compiler_params=_cparams(1),
        name="mlstm_step",
    )(b_i, b_f, zs3, kcol, zs3, zs3, zs3, zgs3, norm_g, state_c, state_n, m0_lanes)


def _merge_out_kernel(a_ref, hb_ref, ga_ref, gb_ref, x_ref, wa_ref, wb_ref, wo_ref, o_ref):
    pa = jnp.dot(a_ref[...], wa_ref[...], preferred_element_type=F32)
    pb = jnp.dot(hb_ref[...], wb_ref[...], preferred_element_type=F32)
    merged = jax.nn.sigmoid(ga_ref[...]) * pa + jax.nn.sigmoid(gb_ref[...]) * pb
    o_ref[...] = x_ref[...] + jnp.dot(merged.astype(BF16), wo_ref[...], preferred_element_type=F32)


def merge_out(a, hb, z, x, w_a, w_b, w_o, layer, tm):
    m, d = x.shape
    once = pl.Buffered(1)
    return pl.pallas_call(
        _merge_out_kernel,
        grid=(m // tm,),
        in_specs=[
            pl.BlockSpec((tm, WA), lambda i: (i, 0)),
            pl.BlockSpec((tm, WB), lambda i: (i, 0)),
            pl.BlockSpec((tm, d), lambda i: (i, 0)),
            pl.BlockSpec((tm, d), lambda i: (i, 1)),
            pl.BlockSpec((tm, d), lambda i: (i, 0)),
            pl.BlockSpec((None, WA, d), lambda i: (layer, 0, 0), pipeline_mode=once),
            pl.BlockSpec((None, WB, d), lambda i: (layer, 0, 0), pipeline_mode=once),
            pl.BlockSpec((None, d, d), lambda i: (layer, 0, 0), pipeline_mode=once),
        ],
        out_specs=pl.BlockSpec((tm, d), lambda i: (i, 0)),
        out_shape=jax.ShapeDtypeStruct((m, d), F32),
        compiler_params=_cparams(1),
        name="merge_out",
    )(a, hb, z, z, x, w_a, w_b, w_o)


def _ffn_kernel(x_ref, g_ref, wg_ref, wu_ref, wd_ref, o_ref, h_ref, acc_ref):
    f = pl.program_id(1)

    @pl.when(f == 0)
    def _():
        h_ref[...] = _rms(x_ref[...], g_ref[...]).astype(BF16)
        acc_ref[...] = jnp.zeros_like(acc_ref)

    h = h_ref[...]
    gate = jnp.dot(h, wg_ref[...], preferred_element_type=F32)
    up = jnp.dot(h, wu_ref[...], preferred_element_type=F32)
    act = (gate * jax.nn.sigmoid(gate) * up).astype(BF16)
    acc_ref[...] += jnp.dot(act, wd_ref[...], preferred_element_type=F32)

    @pl.when(f == pl.num_programs(1) - 1)
    def _():
        o_ref[...] = x_ref[...] + acc_ref[...]


def ffn(x, g, w_gu, w_d, layer, tm, tf):
    m, d = x.shape
    nf = D_FF // tf
    return pl.pallas_call(
        _ffn_kernel,
        grid=(m // tm, nf),
        in_specs=[
            pl.BlockSpec((tm, d), lambda i, f: (i, 0)),
            pl.BlockSpec((None, 1, d), lambda i, f: (layer, 0, 0)),
            pl.BlockSpec((None, d, tf), lambda i, f: (layer, 0, f)),
            pl.BlockSpec((None, d, tf), lambda i, f: (layer, 0, nf + f)),
            pl.BlockSpec((None, tf, d), lambda i, f: (layer, f, 0)),
        ],
        out_specs=pl.BlockSpec((tm, d), lambda i, f: (i, 0)),
        out_shape=jax.ShapeDtypeStruct((m, d), F32),
        scratch_shapes=[pltpu.VMEM((tm, d), BF16), pltpu.VMEM((tm, d), F32)],
        compiler_params=_cparams(2),
        name="ffn",
    )(x, g, w_gu, w_gu, w_d)


def kernel(x_prompt, x_sample, cache_k, cache_v, state_C, state_n, state_m, page_table, norm1_g, w_in, b_igate,
           b_fgate, lam_q1, lam_k1, lam_q2, lam_k2, subln_g, mlstm_norm_g, w_proj_a, w_proj_b, w_out, norm2_g,
           w_gu, w_down, final_g):
    bp, lp, d = x_prompt.shape
    bs = x_sample.shape[0]
    n_pool = cache_k.shape[1]
    n_pages = page_table.shape[1]
    past_len = n_pages * PAGE_SIZE
    n_rows = PAGE_SIZE * HA

    gate_lo = 3 * WA + 4 * WB
    w_tail = w_in[:, :, gate_lo + 2 * HB:].astype(BF16)
    w_head = w_in[:, :, :gate_lo].astype(BF16)
    w_gate = jnp.pad(w_in[:, :, gate_lo:gate_lo + 2 * HB], ((0, 0), (0, 0), (0, GATE_COLS - 2 * HB))).astype(BF16)
    w_a = w_proj_a.astype(BF16)
    w_b = w_proj_b.astype(BF16)
    w_o = w_out.astype(BF16)
    w_gu_b = w_gu.astype(BF16)
    w_d = w_down.astype(BF16)
    norm1 = norm1_g.reshape(DEPTH, 1, d)
    norm2 = norm2_g.reshape(DEPTH, 1, d)
    subln = subln_g.reshape(DEPTH, 1, DV_A)
    subln_col = subln_g.reshape(DEPTH, DV_A, 1)
    mnorm = mlstm_norm_g.reshape(DEPTH, 1, DV_B)

    slopes = jnp.asarray(2.0 ** (-8.0 * np.arange(1, HA + 1) / HA), dtype=F32)
    lam_all = (jnp.exp(jnp.sum(lam_q1 * lam_k1, axis=-1)) - jnp.exp(jnp.sum(lam_q2 * lam_k2, axis=-1))
               + jnp.asarray([0.8 - 0.6 * math.exp(-0.3 * l) for l in range(DEPTH)], dtype=F32))

    key_head = np.arange(n_rows) % HA
    key_tok = np.arange(n_rows) // HA
    q_head = np.arange(2 * HA) % HA
    slopes_np = 2.0 ** (-8.0 * np.arange(1, HA + 1) / HA)
    assert np.all(np.log2(slopes_np) == np.round(np.log2(slopes_np)))
    bias_np = np.where(key_head[None, :] == q_head[:, None], slopes_np[q_head][:, None] * key_tok[None, :], NEG)
    bias_tile = jnp.asarray(bias_np, dtype=F32)
    slope_col = jnp.asarray(slopes_np[q_head][:, None], dtype=F32)
    half_mask = jnp.asarray((np.arange(DV_A)[None, None, :] // QK_HALF) == np.arange(2)[:, None, None])

    cache_k4 = cache_k.reshape(DEPTH, n_pool, n_rows, DV_A)
    cache_v4 = cache_v.reshape(DEPTH, n_pool, n_rows, DV_A)
    pt_flat = page_table.reshape(-1)
    m0_lanes = jnp.broadcast_to(state_m[..., None], state_m.shape + (128,))

    xp = x_prompt.reshape(bp * lp, d)
    xs = x_sample.reshape(bs, d)
    outs = {k: [] for k in ("kp", "vp", "cp", "np", "mp", "ks", "vs", "cs", "ns", "ms")}
    for l in range(DEPTH):
        lam = lam_all[l:l + 1]
        z, zg = in_proj(xp, norm1, w_tail, w_head, w_gate, l, tm=1024, tn=1024)
        z3 = z.reshape(bp, lp, Z_COLS)
        outs["kp"].append(z3[:, :, COL_KA:COL_KA + WA].reshape(bp, lp, HA, DV_A))
        outs["vp"].append(z3[:, :, COL_VA:COL_VA + WA].reshape(bp, lp, HA, DV_A))
        a = attn_prompt(z3, slopes, lam, subln_col, l, tq=512, tk=256)
        hb, c_p, n_p, m_p = mlstm_prompt(z3, zg.reshape(bp, lp, GATE_COLS), b_igate[l], b_fgate[l], mnorm, l,
                                         chunk=256)
        outs["cp"].append(c_p)
        outs["np"].append(n_p)
        outs["mp"].append(m_p[:, :, 0])
        xp = merge_out(a.reshape(bp * lp, WA), hb.reshape(bp * lp, WB), z, xp, w_a, w_b, w_o, l, tm=256)
        xp = ffn(xp, norm2, w_gu_b, w_d, l, tm=512, tf=512)

        zs, zgs = in_proj(xs, norm1, w_tail, w_head, w_gate, l, tm=bs, tn=1024)
        k_new = zs[:, COL_KA:COL_KA + WA].reshape(bs, HA, DV_A)
        v_new = zs[:, COL_VA:COL_VA + WA].reshape(bs, HA, DV_A)
        outs["ks"].append(k_new.reshape(bs, 1, HA, DV_A))
        outs["vs"].append(v_new.reshape(bs, 1, HA, DV_A))
        q_new = zs[:, COL_QA:COL_QA + WA].reshape(bs, 1, HA, DV_A) * (QK_HALF ** -0.5)
        qm = jnp.where(half_mask[None], q_new, 0.0).reshape(bs, 2 * HA, DV_A).astype(BF16)
        kown = jnp.tile(k_new, (1, 2, 1))
        vown = jnp.tile(v_new, (1, 2, 1))
        a_s = attn_decode(qm, kown, vown, bias_tile, slope_col, lam, subln, cache_k4, cache_v4, pt_flat, l,
                          pages_per_step=8)
        zs3 = zs.reshape(bs, 1, Z_COLS)
        kcol = zs[:, COL_KB:COL_KB + WB].reshape(bs, HB, DK, 1)
        hb_s, c_s, n_s, m_s = mlstm_step(zs3, kcol, zgs.reshape(bs, 1, GATE_COLS), b_igate[l], b_fgate[l], mnorm,
                                         state_C, state_n, m0_lanes, l)
        outs["cs"].append(c_s)
        outs["ns"].append(n_s)
        outs["ms"].append(m_s[:, :, 0])
        xs = merge_out(a_s.reshape(bs, WA), hb_s.reshape(bs, WB), zs, xs, w_a, w_b, w_o, l, tm=bs)
        xs = ffn(xs, norm2, w_gu_b, w_d, l, tm=bs, tf=512)

    y_prompt = rmsnorm_rows(xp, final_g, tm=512).reshape(bp, lp, d)
    y_sample = rmsnorm_rows(xs, final_g, tm=bs).reshape(bs, 1, d)
    st = {k: jnp.stack(v) for k, v in outs.items()}
    return (y_prompt, y_sample, st["kp"], st["vp"], st["cp"], st["np"], st["mp"],
            st["ks"], st["vs"], st["cs"], st["ns"], st["ms"])
```

```python
import functools
import math

import numpy as np
import jax
import jax.numpy as jnp
from jax import lax
from jax.experimental import pallas as pl
from jax.experimental.pallas import tpu as pltpu

F32 = jnp.float32
BF16 = jnp.bfloat16

D_MODEL = 2048
DEPTH = 4
PAGE_SIZE = 128
WA = D_MODEL // 2
DV_A = 128
HA = WA // DV_A
QK_HALF = DV_A // 2
WB = D_MODEL // 2
HB = 4
DK = WB // HB
DV_B = WB // HB
D_FF = 5632
EPS = 1e-6
NEG = -1e30

Z_COLS = 2 * D_MODEL + 3 * WA + 4 * WB
COL_QA = 2 * D_MODEL
COL_KA = COL_QA + WA
COL_VA = COL_KA + WA
COL_QB = COL_VA + WA
COL_KB = COL_QB + WB
COL_VB = COL_KB + WB
COL_OB = COL_VB + WB
GATE_COLS = 128

VMEM_LIMIT = 56 * 1024 * 1024


def _cparams(n_axes, vmem=VMEM_LIMIT):
    return pltpu.CompilerParams(dimension_semantics=("arbitrary",) * n_axes, vmem_limit_bytes=vmem)


def _rms(x, g):
    return x * lax.rsqrt(jnp.mean(x * x, axis=-1, keepdims=True) + EPS) * g


def _rmsnorm_kernel(x_ref, g_ref, o_ref):
    o_ref[...] = _rms(x_ref[...], g_ref[...])


def rmsnorm_rows(x, g, tm):
    m, d = x.shape
    return pl.pallas_call(
        _rmsnorm_kernel,
        grid=(m // tm,),
        in_specs=[pl.BlockSpec((tm, d), lambda i: (i, 0)), pl.BlockSpec((1, d), lambda i: (0, 0))],
        out_specs=pl.BlockSpec((tm, d), lambda i: (i, 0)),
        out_shape=jax.ShapeDtypeStruct((m, d), F32),
        compiler_params=_cparams(1),
        name="final_rmsnorm",
    )(x, g.reshape(1, d))


def _inproj_kernel(x_ref, g_ref, wt_ref, wh_ref, wg_ref, z_ref, kv_ref, zg_ref, h_ref, *, n_tail, j_k):
    j = pl.program_id(1)

    @pl.when(j == 0)
    def _():
        h = _rms(x_ref[...], g_ref[...]).astype(BF16)
        h_ref[...] = h
        zg_ref[...] = jnp.dot(h, wg_ref[...], preferred_element_type=F32)

    @pl.when(j < n_tail)
    def _():
        z_ref[...] = jnp.dot(h_ref[...], wt_ref[...], preferred_element_type=F32).astype(z_ref.dtype)

    @pl.when(j >= n_tail)
    def _():
        r = jnp.dot(h_ref[...], wh_ref[...], preferred_element_type=F32)
        z_ref[...] = r.astype(z_ref.dtype)

        @pl.when((j == j_k) | (j == j_k + 1))
        def _():
            kv_ref[...] = r


def in_proj(x, g, w_tail, w_head, w_gate, layer, tm, tn):
    m, d = x.shape
    n_tail = w_tail.shape[-1] // tn
    assert tn == WA and COL_VA == COL_KA + WA
    j_k = COL_KA // tn
    return pl.pallas_call(
        functools.partial(_inproj_kernel, n_tail=n_tail, j_k=j_k),
        grid=(m // tm, Z_COLS // tn),
        in_specs=[
            pl.BlockSpec((tm, d), lambda i, j: (i, 0)),
            pl.BlockSpec((None, 1, d), lambda i, j: (layer, 0, 0)),
            pl.BlockSpec((None, d, tn), lambda i, j: (layer, 0, jnp.minimum(j, n_tail - 1))),
            pl.BlockSpec((None, d, tn), lambda i, j: (layer, 0, jnp.maximum(j - n_tail, 0))),
            pl.BlockSpec((None, d, GATE_COLS), lambda i, j: (layer, 0, 0)),
        ],
        out_specs=[
            pl.BlockSpec((tm, tn), lambda i, j: (i, j)),
            pl.BlockSpec((tm, WA), lambda i, j: (i, jnp.clip(j - j_k, 0, 1))),
            pl.BlockSpec((tm, GATE_COLS), lambda i, j: (i, 0)),
        ],
        out_shape=[jax.ShapeDtypeStruct((m, Z_COLS), BF16), jax.ShapeDtypeStruct((m, 2 * WA), F32),
                   jax.ShapeDtypeStruct((m, GATE_COLS), F32)],
        scratch_shapes=[pltpu.VMEM((tm, d), BF16)],
        compiler_params=_cparams(2),
        name="in_proj",
    )(x, g, w_tail, w_head, w_gate)


def _kv_layout_kernel(*refs, tm):
    kv_refs, (ko_ref, vo_ref) = refs[:-2], refs[-2:]
    layer = pl.program_id(0)
    for a, kv_ref in enumerate(kv_refs):
        @pl.when(layer == a)
        def _(kv_ref=kv_ref):
            for h in range(HA):
                ko_ref[pl.ds(h, tm, stride=HA), :] = kv_ref[:, h * DV_A:(h + 1) * DV_A]
                vo_ref[pl.ds(h, tm, stride=HA), :] = kv_ref[:, WA + h * DV_A:WA + (h + 1) * DV_A]


def kv_layout(kvs, tm):
    n_layers = len(kvs)
    m = kvs[0].shape[0]
    n_i = m // tm

    def kv_spec(a):
        return pl.BlockSpec((tm, 2 * WA), lambda l, i: (jnp.where(l == a, i, jnp.where(l < a, 0, n_i - 1)), 0))

    out = jax.ShapeDtypeStruct((n_layers, m * HA, DV_A), F32)
    return pl.pallas_call(
        functools.partial(_kv_layout_kernel, tm=tm),
        grid=(n_layers, n_i),
        in_specs=[kv_spec(a) for a in range(n_layers)],
        out_specs=[pl.BlockSpec((None, tm * HA, DV_A), lambda l, i: (l, i, 0))] * 2,
        out_shape=[out, out],
        compiler_params=_cparams(2),
        name="kv_layout",
    )(*kvs)


def _attn_prompt_kernel(slope_ref, lam_ref, q_ref, k_ref, v_ref, g_ref, o_ref, kb_ref, vt_ref, s_ref, m_ref, acc_ref,
                        *, tq, tk, out_scale):
    h = pl.program_id(1)
    qi = pl.program_id(2)
    n_blocks = vt_ref.shape[0]
    n_sub = tq // tk

    @pl.when(qi == 0)
    def _():
        seq = k_ref.shape[0]
        kpos = lax.broadcasted_iota(jnp.int32, (seq, DV_A), 0)
        flane = lax.broadcasted_iota(jnp.int32, (seq, DV_A), 1)
        kfeat = jnp.where(flane == 0, jnp.bitwise_and(kpos, tk - 1).astype(F32), jnp.where(flane <= 2, 1.0, 0.0))
        kb_ref[:, :DV_A] = k_ref[...].astype(BF16)
        kb_ref[:, DV_A:] = kfeat.astype(BF16)
        for jb in range(n_blocks):
            vt_ref[jb, :DV_A, :] = v_ref[jb * tk:(jb + 1) * tk, :].astype(F32).T.astype(BF16)
            vt_ref[jb, DV_A:, :] = jnp.ones((vt_ref.shape[1] - DV_A, tk), BF16)

    slope = slope_ref[h]
    lam = lam_ref[0]
    q = q_ref[...].astype(F32) * (QK_HALF ** -0.5)
    lane = lax.broadcasted_iota(jnp.int32, (tq, DV_A), 1)
    qs = jnp.concatenate([jnp.where(lane < QK_HALF, q, 0.0), jnp.where(lane >= QK_HALF, q, 0.0)], axis=0)
    qpos = jnp.bitwise_and(lax.broadcasted_iota(jnp.int32, (2 * tq, DV_A), 0), tq - 1)
    flane = lax.broadcasted_iota(jnp.int32, (2 * tq, DV_A), 1)
    qfeat = jnp.where(flane == 0, slope,
                      jnp.where(flane == 1, -slope * jnp.bitwise_and(qpos, 255).astype(F32),
                                jnp.where(flane == 2, -slope * (qpos - jnp.bitwise_and(qpos, 255)).astype(F32), 0.0)))
    qaug = jnp.concatenate([qs, qfeat], axis=1).astype(BF16)

    def scores(jb):
        kj = kb_ref[pl.ds(pl.multiple_of(jb * tk, tk), tk), :]
        return lax.dot_general(kj, qaug, (((1,), (1,)), ((), ())), preferred_element_type=F32)

    def softmax_pv(jb, s, off):
        m = m_ref[...]
        m_new = jnp.maximum(m, jnp.max(s, axis=0, keepdims=True) - off)
        alpha = jnp.exp(m - m_new)
        p = jnp.exp(s - (m_new + off))
        acc_ref[...] = alpha * acc_ref[...] + jnp.dot(vt_ref[jb], p.astype(BF16), preferred_element_type=F32)
        m_ref[...] = m_new

    def causal(s, shift):
        krow = lax.broadcasted_iota(jnp.int32, (tk, 2 * tq), 0)
        qcol = jnp.bitwise_and(lax.broadcasted_iota(jnp.int32, (tk, 2 * tq), 1), tq - 1)
        return jnp.where(krow + shift <= qcol, s, NEG)

    def full_blocks(jj, carry):
        for u in range(n_sub):
            jb = jj * n_sub + u
            s_ref[(u + 1) % 2] = scores(jb + 1)
            softmax_pv(jb, s_ref[u % 2], slope * (qi * tq - jb * tk).astype(F32))
        return carry

    s_ref[0] = scores(0)
    m_ref[...] = jnp.full(m_ref.shape, NEG, F32)
    acc_ref[...] = jnp.zeros(acc_ref.shape, F32)
    lax.fori_loop(0, qi, full_blocks, 0)
    for u in range(n_sub):
        jb = qi * n_sub + u
        if u + 1 < n_sub:
            s_ref[(u + 1) % 2] = scores(jb + 1)
        softmax_pv(jb, causal(s_ref[u % 2], u * tk), slope * (-u * tk))
    o = acc_ref[:DV_A, :] / acc_ref[DV_A:DV_A + 1, :]
    a = o[:, :tq] - lam * o[:, tq:]
    y = a * lax.rsqrt(jnp.mean(a * a, axis=0, keepdims=True) + EPS) * g_ref[...] * out_scale
    o_ref[...] = y.T.astype(o_ref.dtype)


def attn_prompt(z3, slopes, lam, subln_col, layer, tq, tk):
    b, seq, _ = z3.shape
    assert tk <= 256 and tq == 2 * tk and tq & (tq - 1) == 0 and tk & (tk - 1) == 0
    lam_init = 0.8 - 0.6 * math.exp(-0.3 * layer)
    kern = functools.partial(_attn_prompt_kernel, tq=tq, tk=tk, out_scale=1.0 - lam_init)
    return pl.pallas_call(
        kern,
        grid=(b, HA, seq // tq),
        in_specs=[
            pl.BlockSpec(memory_space=pltpu.SMEM),
            pl.BlockSpec(memory_space=pltpu.SMEM),
            pl.BlockSpec((None, tq, DV_A), lambda bi, h, qi: (bi, qi, COL_QA // DV_A + h)),
            pl.BlockSpec((None, seq, DV_A), lambda bi, h, qi: (bi, 0, COL_KA // DV_A + h)),
            pl.BlockSpec((None, seq, DV_A), lambda bi, h, qi: (bi, 0, COL_VA // DV_A + h)),
            pl.BlockSpec((None, DV_A, 1), lambda bi, h, qi: (layer, 0, 0)),
        ],
        out_specs=pl.BlockSpec((None, tq, DV_A), lambda bi, h, qi: (bi, qi, h)),
        out_shape=jax.ShapeDtypeStruct((b, seq, WA), BF16),
        scratch_shapes=[pltpu.VMEM((seq, 2 * DV_A), BF16), pltpu.VMEM((seq // tk, DV_A + 16, tk), BF16),
                        pltpu.VMEM((2, tk, 2 * tq), F32), pltpu.VMEM((1, 2 * tq), F32),
                        pltpu.VMEM((DV_A + 16, 2 * tq), F32)],
        compiler_params=_cparams(3),
        name="attn_prompt",
    )(slopes, lam, z3, z3, z3, subln_col)


def _attn_decode_kernel(pt_ref, lam_ref, qm_ref, kown_ref, vown_ref, bias_ref, slope_ref, g_ref, *rest,
                        pages_per_step, past_len, out_scale):
    del pt_ref
    k_refs = rest[:pages_per_step]
    v_refs = rest[pages_per_step:2 * pages_per_step]
    o_ref, m_ref, l_ref, acc_ref = rest[2 * pages_per_step:]
    i = pl.program_id(1)
    qm = qm_ref[...]

    @pl.when(i == 0)
    def _():
        ko = kown_ref[...].astype(BF16).astype(F32)
        m_ref[...] = jnp.sum(qm.astype(F32) * ko, axis=-1, keepdims=True)
        l_ref[...] = jnp.ones_like(l_ref)
        acc_ref[...] = vown_ref[...].astype(BF16).astype(F32)

    ss = []
    for gi in range(pages_per_step):
        kp = k_refs[gi][...].astype(BF16)
        s = lax.dot_general(qm, kp, (((1,), (1,)), ((), ())), preferred_element_type=F32)
        page_pos = i * pages_per_step + gi
        off = (page_pos * PAGE_SIZE - past_len).astype(F32)
        ss.append(s + (bias_ref[...] + slope_ref[...] * off))
    s = jnp.concatenate(ss, axis=-1)
    m_old = m_ref[...]
    m_new = jnp.maximum(m_old, jnp.max(s, axis=-1, keepdims=True))
    alpha = jnp.exp(m_old - m_new)
    p = jnp.exp(s - m_new)
    l_ref[...] = alpha * l_ref[...] + jnp.sum(p, axis=-1, keepdims=True)
    pv = jnp.zeros(acc_ref.shape, F32)
    n_rows = PAGE_SIZE * HA
    for gi in range(pages_per_step):
        vp = v_refs[gi][...].astype(BF16)
        pv = pv + jnp.dot(p[:, gi * n_rows:(gi + 1) * n_rows].astype(BF16), vp, preferred_element_type=F32)
    acc_ref[...] = alpha * acc_ref[...] + pv
    m_ref[...] = m_new

    @pl.when(i == pl.num_programs(1) - 1)
    def _():
        o = acc_ref[...] / l_ref[...]
        a = o[:HA] - lam_ref[0] * o[HA:]
        o_ref[...] = (_rms(a, g_ref[...]) * out_scale).astype(o_ref.dtype)


def attn_decode(qm, kown, vown, bias_tile, slope_col, lam, subln_g, cache_k4, cache_v4, page_table_flat, layer,
                pages_per_step):
    bs = qm.shape[0]
    n_pages = page_table_flat.shape[0] // bs
    n_rows = PAGE_SIZE * HA
    lam_init = 0.8 - 0.6 * math.exp(-0.3 * layer)
    kern = functools.partial(_attn_decode_kernel, pages_per_step=pages_per_step,
                             past_len=n_pages * PAGE_SIZE, out_scale=1.0 - lam_init)

    def page_spec(gi):
        return pl.BlockSpec((None, None, n_rows, DV_A),
                            lambda b, i, pt: (layer, pt[b * n_pages + i * pages_per_step + gi], 0, 0))

    grid_spec = pltpu.PrefetchScalarGridSpec(
        num_scalar_prefetch=1,
        grid=(bs, n_pages // pages_per_step),
        in_specs=[
            pl.BlockSpec(memory_space=pltpu.SMEM),
            pl.BlockSpec((None, 2 * HA, DV_A), lambda b, i, pt: (b, 0, 0)),
            pl.BlockSpec((None, 2 * HA, DV_A), lambda b, i, pt: (b, 0, 0)),
            pl.BlockSpec((None, 2 * HA, DV_A), lambda b, i, pt: (b, 0, 0)),
            pl.BlockSpec((2 * HA, n_rows), lambda b, i, pt: (0, 0)),
            pl.BlockSpec((2 * HA, 1), lambda b, i, pt: (0, 0)),
            pl.BlockSpec((None, 1, DV_A), lambda b, i, pt: (layer, 0, 0)),
        ] + [page_spec(gi) for gi in range(pages_per_step)] * 2,
        out_specs=pl.BlockSpec((None, HA, DV_A), lambda b, i, pt: (b, 0, 0)),
        scratch_shapes=[pltpu.VMEM((2 * HA, 1), F32), pltpu.VMEM((2 * HA, 1), F32),
                        pltpu.VMEM((2 * HA, DV_A), F32)],
    )
    return pl.pallas_call(
        kern,
        grid_spec=grid_spec,
        out_shape=jax.ShapeDtypeStruct((bs, HA, DV_A), BF16),
        compiler_params=_cparams(2),
        name="attn_decode",
    )(page_table_flat, lam, qm, kown, vown, bias_tile, slope_col, subln_g,
      *([cache_k4] * pages_per_step), *([cache_v4] * pages_per_step))


def _log_sigmoid(x):
    return jnp.minimum(x, 0.0) - jnp.log1p(jnp.exp(-jnp.abs(x)))


def _mlstm_prompt_kernel(bi_ref, bf_ref, q_ref, k_ref, v_ref, ob_ref, zg_ref, g_ref,
                         hb_ref, c_out_ref, n_out_ref, m_out_ref, c_ref, n_ref, m_ref, *, chunk):
    ci = pl.program_id(1)

    @pl.when(ci == 0)
    def _():
        c_ref[...] = jnp.zeros_like(c_ref)
        n_ref[...] = jnp.zeros_like(n_ref)
        m_ref[...] = jnp.zeros_like(m_ref)

    gates = zg_ref[...]
    gates_t = gates.T
    row = lax.broadcasted_iota(jnp.int32, (chunk, chunk), 0)
    col = lax.broadcasted_iota(jnp.int32, (chunk, chunk), 1)
    causal = col <= row
    g_norm = g_ref[...]
    for h in range(HB):
        sl = slice(h * DK, (h + 1) * DK)
        i_col = gates[:, h:h + 1] + bi_ref[h]
        i_row = gates_t[h:h + 1, :] + bi_ref[h]
        lf_col = _log_sigmoid(gates[:, HB + h:HB + h + 1] + bf_ref[h])
        lf_row = _log_sigmoid(gates_t[HB + h:HB + h + 1, :] + bf_ref[h])
        b_col = jnp.sum(jnp.where(causal, lf_row, 0.0), axis=1, keepdims=True)
        b_row = jnp.sum(jnp.where(row <= col, lf_col, 0.0), axis=0, keepdims=True)
        m_prev = m_ref[h:h + 1, 0:1]
        g_col = b_col + m_prev
        dmat = jnp.where(causal, b_col - b_row + i_row, NEG)
        m_t = jnp.maximum(g_col, jnp.max(dmat, axis=1, keepdims=True))
        w_inter = jnp.exp(g_col - m_t)
        wmat = jnp.exp(dmat - m_t)
        qs = (q_ref[:, sl].astype(F32) * (DK ** -0.5)).astype(BF16)
        kb = k_ref[:, sl].astype(BF16)
        vb = v_ref[:, sl].astype(BF16)
        c_old = c_ref[h]
        n_old = n_ref[h:h + 1, :]
        s = lax.dot_general(qs, kb, (((1,), (1,)), ((), ())), preferred_element_type=F32) * wmat
        num = (w_inter * jnp.dot(qs, c_old.astype(BF16), preferred_element_type=F32)
               + jnp.dot(s.astype(BF16), vb, preferred_element_type=F32))
        qn = jnp.sum(qs.astype(F32) * n_old.astype(BF16).astype(F32), axis=-1, keepdims=True)
        den = w_inter * qn + jnp.sum(s, axis=-1, keepdims=True)
        hv = num / jnp.maximum(jnp.abs(den), jnp.exp(-m_t))
        hb = _rms(hv, g_norm) * jax.nn.sigmoid(ob_ref[:, sl].astype(F32))
        hb_ref[:, sl] = hb.astype(hb_ref.dtype)
        m_last = m_t[chunk - 1:chunk, :]
        w_last = w_inter[chunk - 1:chunk, :]
        wc_col = jnp.exp(b_col[chunk - 1:chunk, :] - b_col + i_col - m_last)
        kw = kb.astype(F32) * wc_col.astype(BF16).astype(F32)
        c_new = w_last * c_old + jnp.dot(kw.T.astype(BF16), vb, preferred_element_type=F32)
        n_new = w_last * n_old + jnp.sum(kw, axis=0, keepdims=True)
        c_ref[h] = c_new
        n_ref[h:h + 1, :] = n_new
        m_ref[h:h + 1, :] = jnp.broadcast_to(m_last, (1, m_ref.shape[1]))

    @pl.when(ci == pl.num_programs(1) - 1)
    def _():
        c_out_ref[...] = c_ref[...]
        n_out_ref[...] = n_ref[...]
        m_out_ref[...] = m_ref[...]


def mlstm_prompt(z3, zg3, b_i, b_f, norm_g, layer, chunk):
    b, seq, _ = z3.shape

    def zspec(col):
        return pl.BlockSpec((None, chunk, WB), lambda bi, ci: (bi, ci, col // WB))

    return pl.pallas_call(
        functools.partial(_mlstm_prompt_kernel, chunk=chunk),
        grid=(b, seq // chunk),
        in_specs=[
            pl.BlockSpec(memory_space=pltpu.SMEM),
            pl.BlockSpec(memory_space=pltpu.SMEM),
            zspec(COL_QB), zspec(COL_KB), zspec(COL_VB), zspec(COL_OB),
            pl.BlockSpec((None, chunk, GATE_COLS), lambda bi, ci: (bi, ci, 0)),
            pl.BlockSpec((None, 1, DV_B), lambda bi, ci: (layer, 0, 0)),
        ],
        out_specs=[
            pl.BlockSpec((None, chunk, WB), lambda bi, ci: (bi, ci, 0)),
            pl.BlockSpec((None, HB, DK, DV_B), lambda bi, ci: (bi, 0, 0, 0)),
            pl.BlockSpec((None, HB, DK), lambda bi, ci: (bi, 0, 0)),
            pl.BlockSpec((None, HB, 128), lambda bi, ci: (bi, 0, 0)),
        ],
        out_shape=[
            jax.ShapeDtypeStruct((b, seq, WB), BF16),
            jax.ShapeDtypeStruct((b, HB, DK, DV_B), F32),
            jax.ShapeDtypeStruct((b, HB, DK), F32),
            jax.ShapeDtypeStruct((b, HB, 128), F32),
        ],
        scratch_shapes=[pltpu.VMEM((HB, DK, DV_B), F32), pltpu.VMEM((HB, DK), F32), pltpu.VMEM((HB, 128), F32)],
        compiler_params=_cparams(2),
        name="mlstm_prompt",
    )(b_i, b_f, z3, z3, z3, z3, zg3, norm_g)


def _mlstm_step_kernel(bi_ref, bf_ref, q_ref, kcol_ref, krow_ref, v_ref, ob_ref, zg_ref, g_ref,
                       c0_ref, n0_ref, m0_ref, hb_ref, c_out_ref, n_out_ref, m_out_ref):
    gates = zg_ref[...]
    g_norm = g_ref[...]
    for h in range(HB):
        sl = slice(h * DK, (h + 1) * DK)
        i_pre = gates[:, h:h + 1] + bi_ref[h]
        lf = _log_sigmoid(gates[:, HB + h:HB + h + 1] + bf_ref[h])
        m0 = m0_ref[h:h + 1, 0:1]
        g_st = lf + m0
        m_t = jnp.maximum(g_st, i_pre)
        w_inter = jnp.exp(g_st - m_t)
        w_in = jnp.exp(i_pre - m_t)
        qs = (q_ref[:, sl] * (DK ** -0.5)).astype(BF16)
        k_row = krow_ref[:, sl].astype(BF16).astype(F32)
        k_col = kcol_ref[h].astype(BF16).astype(F32)
        v_row = v_ref[:, sl].astype(BF16).astype(F32)
        c_old = c0_ref[h]
        n_old = n0_ref[h:h + 1, :]
        qf = qs.astype(F32)
        s = jnp.sum(qf * k_row, axis=-1, keepdims=True) * w_in
        q16 = jnp.broadcast_to(qs, (16, DK))
        qc = jnp.dot(q16, c_old.astype(BF16), preferred_element_type=F32)[0:1, :]
        num = w_inter * qc + s.astype(BF16).astype(F32) * v_row
        qn = jnp.sum(qf * n_old.astype(BF16).astype(F32), axis=-1, keepdims=True)
        den = w_inter * qn + s
        hv = num / jnp.maximum(jnp.abs(den), jnp.exp(-m_t))
        hb = _rms(hv, g_norm) * jax.nn.sigmoid(ob_ref[:, sl])
        hb_ref[:, sl] = hb.astype(hb_ref.dtype)
        wb = w_in.astype(BF16).astype(F32)
        kw_col = (k_col * wb).astype(BF16).astype(F32)
        c_out_ref[h] = w_inter * c_old + kw_col * v_row
        n_out_ref[h:h + 1, :] = w_inter * n_old + wb * k_row
        m_out_ref[h:h + 1, :] = jnp.broadcast_to(m_t, (1, m_out_ref.shape[1]))


def mlstm_step(zs3, kcol, zgs3, b_i, b_f, norm_g, state_c, state_n, m0_lanes, layer):
    bs = zs3.shape[0]

    def zspec(col):
        return pl.BlockSpec((None, 1, WB), lambda bi: (bi, 0, col // WB))

    return pl.pallas_call(
        _mlstm_step_kernel,
        grid=(bs,),
        in_specs=[
            pl.BlockSpec(memory_space=pltpu.SMEM),
            pl.BlockSpec(memory_space=pltpu.SMEM),
            zspec(COL_QB),
            pl.BlockSpec((None, HB, DK, 1), lambda bi: (bi, 0, 0, 0)),
            zspec(COL_KB), zspec(COL_VB), zspec(COL_OB),
            pl.BlockSpec((None, 1, GATE_COLS), lambda bi: (bi, 0, 0)),
            pl.BlockSpec((None, 1, DV_B), lambda bi: (layer, 0, 0)),
            pl.BlockSpec((None, None, HB, DK, DV_B), lambda bi: (layer, bi, 0, 0, 0)),
            pl.BlockSpec((None, None, HB, DK), lambda bi: (layer, bi, 0, 0)),
            pl.BlockSpec((None, None, HB, 128), lambda bi: (layer, bi, 0, 0)),
        ],
        out_specs=[
            pl.BlockSpec((None, 1, WB), lambda bi: (bi, 0, 0)),
            pl.BlockSpec((None, HB, DK, DV_B), lambda bi: (bi, 0, 0, 0)),
            pl.BlockSpec((None, HB, DK), lambda bi: (bi, 0, 0)),
            pl.BlockSpec((None, HB, 128), lambda bi: (bi, 0, 0)),
        ],
        out_shape=[
            jax.ShapeDtypeStruct((bs, 1, WB), BF16),
            jax.ShapeDtypeStruct((bs, HB, DK, DV_B), F32),
            jax.ShapeDtypeStruct((bs, HB, DK), F32),
            jax.ShapeDtypeStruct((bs, HB, 128), F32),
        ],
        compiler_params=_cparams(1),
        name="mlstm_step",
    )(b_i, b_f, zs3, kcol, zs3, zs3, zs3, zgs3, norm_g, state_c, state_n, m0_lanes)


def _merge_out_kernel(a_ref, hb_ref, ga_ref, gb_ref, x_ref, wa_ref, wb_ref, wo_ref, o_ref):
    pa = jnp.dot(a_ref[...], wa_ref[...], preferred_element_type=F32)
    pb = jnp.dot(hb_ref[...], wb_ref[...], preferred_element_type=F32)
    merged = jax.nn.sigmoid(ga_ref[...].astype(F32)) * pa + jax.nn.sigmoid(gb_ref[...].astype(F32)) * pb
    o_ref[...] = x_ref[...] + jnp.dot(merged.astype(BF16), wo_ref[...], preferred_element_type=F32)


def merge_out(a, hb, z, x, w_a, w_b, w_o, layer, tm):
    m, d = x.shape
    once = pl.Buffered(1)
    return pl.pallas_call(
        _merge_out_kernel,
        grid=(m // tm,),
        in_specs=[
            pl.BlockSpec((tm, WA), lambda i: (i, 0)),
            pl.BlockSpec((tm, WB), lambda i: (i, 0)),
            pl.BlockSpec((tm, d), lambda i: (i, 0)),
            pl.BlockSpec((tm, d), lambda i: (i, 1)),
            pl.BlockSpec((tm, d), lambda i: (i, 0)),
            pl.BlockSpec((None, WA, d), lambda i: (layer, 0, 0), pipeline_mode=once),
            pl.BlockSpec((None, WB, d), lambda i: (layer, 0, 0), pipeline_mode=once),
            pl.BlockSpec((None, d, d), lambda i: (layer, 0, 0), pipeline_mode=once),
        ],
        out_specs=pl.BlockSpec((tm, d), lambda i: (i, 0)),
        out_shape=jax.ShapeDtypeStruct((m, d), F32),
        compiler_params=_cparams(1),
        name="merge_out",
    )(a, hb, z, z, x, w_a, w_b, w_o)


def _ffn_kernel(x_ref, g_ref, wg_ref, wu_ref, wd_ref, o_ref, h_ref, acc_ref):
    f = pl.program_id(1)

    @pl.when(f == 0)
    def _():
        h_ref[...] = _rms(x_ref[...], g_ref[...]).astype(BF16)
        acc_ref[...] = jnp.zeros_like(acc_ref)

    h = h_ref[...]
    gate = jnp.dot(h, wg_ref[...], preferred_element_type=F32)
    up = jnp.dot(h, wu_ref[...], preferred_element_type=F32)
    act = (gate * jax.nn.sigmoid(gate) * up).astype(BF16)
    acc_ref[...] += jnp.dot(act, wd_ref[...], preferred_element_type=F32)

    @pl.when(f == pl.num_programs(1) - 1)
    def _():
        o_ref[...] = x_ref[...] + acc_ref[...]


def ffn(x, g, w_gu, w_d, layer, tm, tf):
    m, d = x.shape
    nf = D_FF // tf
    return pl.pallas_call(
        _ffn_kernel,
        grid=(m // tm, nf),
        in_specs=[
            pl.BlockSpec((tm, d), lambda i, f: (i, 0)),
            pl.BlockSpec((None, 1, d), lambda i, f: (layer, 0, 0)),
            pl.BlockSpec((None, d, tf), lambda i, f: (layer, 0, f)),
            pl.BlockSpec((None, d, tf), lambda i, f: (layer, 0, nf + f)),
            pl.BlockSpec((None, tf, d), lambda i, f: (layer, f, 0)),
        ],
        out_specs=pl.BlockSpec((tm, d), lambda i, f: (i, 0)),
        out_shape=jax.ShapeDtypeStruct((m, d), F32),
        scratch_shapes=[pltpu.VMEM((tm, d), BF16), pltpu.VMEM((tm, d), F32)],
        compiler_params=_cparams(2),
        name="ffn",
    )(x, g, w_gu, w_gu, w_d)


def kernel(x_prompt, x_sample, cache_k, cache_v, state_C, state_n, state_m, page_table, norm1_g, w_in, b_igate,
           b_fgate, lam_q1, lam_k1, lam_q2, lam_k2, subln_g, mlstm_norm_g, w_proj_a, w_proj_b, w_out, norm2_g,
           w_gu, w_down, final_g):
    bp, lp, d = x_prompt.shape
    bs = x_sample.shape[0]
    n_pool = cache_k.shape[1]
    n_pages = page_table.shape[1]
    past_len = n_pages * PAGE_SIZE
    n_rows = PAGE_SIZE * HA

    gate_lo = 3 * WA + 4 * WB
    w_head = w_in.astype(BF16)
    w_tail = w_head[:, :, gate_lo + 2 * HB:]
    w_gate = jnp.pad(w_in[:, :, gate_lo:gate_lo + 2 * HB], ((0, 0), (0, 0), (0, GATE_COLS - 2 * HB))).astype(BF16)
    w_a = w_proj_a.astype(BF16)
    w_b = w_proj_b.astype(BF16)
    w_o = w_out.astype(BF16)
    w_gu_b = w_gu.astype(BF16)
    w_d = w_down.astype(BF16)
    norm1 = norm1_g.reshape(DEPTH, 1, d)
    norm2 = norm2_g.reshape(DEPTH, 1, d)
    subln = subln_g.reshape(DEPTH, 1, DV_A)
    subln_col = subln_g.reshape(DEPTH, DV_A, 1)
    mnorm = mlstm_norm_g.reshape(DEPTH, 1, DV_B)

    slopes = jnp.asarray(2.0 ** (-8.0 * np.arange(1, HA + 1) / HA), dtype=F32)
    lam_all = (jnp.exp(jnp.sum(lam_q1 * lam_k1, axis=-1)) - jnp.exp(jnp.sum(lam_q2 * lam_k2, axis=-1))
               + jnp.asarray([0.8 - 0.6 * math.exp(-0.3 * l) for l in range(DEPTH)], dtype=F32))

    key_head = np.arange(n_rows) % HA
    key_tok = np.arange(n_rows) // HA
    q_head = np.arange(2 * HA) % HA
    slopes_np = 2.0 ** (-8.0 * np.arange(1, HA + 1) / HA)
    assert np.all(np.log2(slopes_np) == np.round(np.log2(slopes_np)))
    bias_np = np.where(key_head[None, :] == q_head[:, None], slopes_np[q_head][:, None] * key_tok[None, :], NEG)
    bias_tile = jnp.asarray(bias_np, dtype=F32)
    slope_col = jnp.asarray(slopes_np[q_head][:, None], dtype=F32)
    half_mask = jnp.asarray((np.arange(DV_A)[None, None, :] // QK_HALF) == np.arange(2)[:, None, None])

    cache_k4 = cache_k.reshape(DEPTH, n_pool, n_rows, DV_A)
    cache_v4 = cache_v.reshape(DEPTH, n_pool, n_rows, DV_A)
    pt_flat = page_table.reshape(-1)
    m0_lanes = jnp.broadcast_to(state_m[..., None], state_m.shape + (128,))

    xp = x_prompt.reshape(bp * lp, d)
    xs = x_sample.reshape(bs, d)
    outs = {k: [] for k in ("cp", "np", "mp", "ks", "vs", "cs", "ns", "ms")}
    kv_layers = []
    for l in range(DEPTH):
        lam = lam_all[l:l + 1]
        z, kv, zg = in_proj(xp, norm1, w_tail, w_head, w_gate, l, tm=1024, tn=1024)
        z3 = z.reshape(bp, lp, Z_COLS)
        kv_layers.append(kv)
        a = attn_prompt(z3, slopes, lam, subln_col, l, tq=512, tk=256)
        hb, c_p, n_p, m_p = mlstm_prompt(z3, zg.reshape(bp, lp, GATE_COLS), b_igate[l], b_fgate[l], mnorm, l,
                                         chunk=256)
        outs["cp"].append(c_p)
        outs["np"].append(n_p)
        outs["mp"].append(m_p[:, :, 0])
        xp = merge_out(a.reshape(bp * lp, WA), hb.reshape(bp * lp, WB), z, xp, w_a, w_b, w_o, l, tm=256)
        xp = ffn(xp, norm2, w_gu_b, w_d, l, tm=512, tf=512)

        zs_b, kvs, zgs = in_proj(xs, norm1, w_tail, w_head, w_gate, l, tm=bs, tn=1024)
        zs = zs_b.astype(F32)
        k_new = kvs[:, :WA].reshape(bs, HA, DV_A)
        v_new = kvs[:, WA:].reshape(bs, HA, DV_A)
        outs["ks"].append(k_new.reshape(bs, 1, HA, DV_A))
        outs["vs"].append(v_new.reshape(bs, 1, HA, DV_A))
        q_new = zs[:, COL_QA:COL_QA + WA].reshape(bs, 1, HA, DV_A) * (QK_HALF ** -0.5)
        qm = jnp.where(half_mask[None], q_new, 0.0).reshape(bs, 2 * HA, DV_A).astype(BF16)
        kown = jnp.tile(k_new, (1, 2, 1))
        vown = jnp.tile(v_new, (1, 2, 1))
        a_s = attn_decode(qm, kown, vown, bias_tile, slope_col, lam, subln, cache_k4, cache_v4, pt_flat, l,
                          pages_per_step=8)
        zs3 = zs.reshape(bs, 1, Z_COLS)
        kcol = zs[:, COL_KB:COL_KB + WB].reshape(bs, HB, DK, 1)
        hb_s, c_s, n_s, m_s = mlstm_step(zs3, kcol, zgs.reshape(bs, 1, GATE_COLS), b_igate[l], b_fgate[l], mnorm,
                                         state_C, state_n, m0_lanes, l)
        outs["cs"].append(c_s)
        outs["ns"].append(n_s)
        outs["ms"].append(m_s[:, :, 0])
        xs = merge_out(a_s.reshape(bs, WA), hb_s.reshape(bs, WB), zs_b, xs, w_a, w_b, w_o, l, tm=bs)
        xs = ffn(xs, norm2, w_gu_b, w_d, l, tm=bs, tf=512)

    y_prompt = rmsnorm_rows(xp, final_g, tm=512).reshape(bp, lp, d)
    y_sample = rmsnorm_rows(xs, final_g, tm=bs).reshape(bs, 1, d)
    st = {k: jnp.stack(v) for k, v in outs.items()}
    k_all, v_all = kv_layout(kv_layers, tm=256)
    st["kp"] = k_all.reshape(DEPTH, bp, lp, HA, DV_A)
    st["vp"] = v_all.reshape(DEPTH, bp, lp, HA, DV_A)
    return (y_prompt, y_sample, st["kp"], st["vp"], st["cp"], st["np"], st["mp"],
            st["ks"], st["vs"], st["cs"], st["ns"], st["ms"])
```

```python
import functools
import math

import numpy as np
import jax
import jax.numpy as jnp
from jax import lax
from jax.experimental import pallas as pl
from jax.experimental.pallas import tpu as pltpu

F32 = jnp.float32
BF16 = jnp.bfloat16

D_MODEL = 2048
DEPTH = 4
PAGE_SIZE = 128
WA = D_MODEL // 2
DV_A = 128
HA = WA // DV_A
QK_HALF = DV_A // 2
WB = D_MODEL // 2
HB = 4
DK = WB // HB
DV_B = WB // HB
D_FF = 5632
EPS = 1e-6
NEG = -1e30

Z_COLS = 2 * D_MODEL + 3 * WA + 4 * WB
COL_QA = 2 * D_MODEL
COL_KA = COL_QA + WA
COL_VA = COL_KA + WA
COL_QB = COL_VA + WA
COL_KB = COL_QB + WB
COL_VB = COL_KB + WB
COL_OB = COL_VB + WB
GATE_COLS = 128

VMEM_LIMIT = 56 * 1024 * 1024


def _cparams(n_axes, vmem=VMEM_LIMIT):
    return pltpu.CompilerParams(dimension_semantics=("arbitrary",) * n_axes, vmem_limit_bytes=vmem)


def _rms(x, g):
    return x * lax.rsqrt(jnp.mean(x * x, axis=-1, keepdims=True) + EPS) * g


def _cast_w_in_kernel(a_ref, b_ref, wt_ref, wh_ref, wg_ref, *, n_tail, shift):
    j = pl.program_id(1)
    tn = a_ref.shape[1]

    @pl.when(j < n_tail)
    def _():
        x = jnp.concatenate([a_ref[...], b_ref[...]], axis=1)
        wt_ref[...] = pltpu.roll(x, x.shape[1] - shift, axis=1)[:, :tn].astype(BF16)

    @pl.when(j == 0)
    def _():
        lane = lax.broadcasted_iota(jnp.int32, (a_ref.shape[0], GATE_COLS), 1)
        wg_ref[...] = jnp.where(lane < shift, a_ref[:, :GATE_COLS], 0.0).astype(BF16)

    @pl.when(j >= n_tail)
    def _():
        wh_ref[...] = a_ref[...].astype(BF16)


def cast_w_in(w_in, tn):
    n_layers, d, n_in = w_in.shape
    gate_lo = 3 * WA + 4 * WB
    shift = 2 * HB
    n_head, n_tail = gate_lo // tn, (n_in - gate_lo - shift) // tn
    assert gate_lo % tn == 0 and n_in == gate_lo + shift + n_tail * tn and tn % GATE_COLS == 0
    sub = tn // GATE_COLS
    return pl.pallas_call(
        functools.partial(_cast_w_in_kernel, n_tail=n_tail, shift=shift),
        grid=(n_layers, n_tail + n_head),
        in_specs=[
            pl.BlockSpec((None, d, tn), lambda l, j: (l, 0, jnp.where(j < n_tail, n_head + j, j - n_tail))),
            pl.BlockSpec((None, d, GATE_COLS),
                         lambda l, j: (l, 0, jnp.where(j < n_tail, (n_head + j + 1) * sub, 0))),
        ],
        out_specs=[
            pl.BlockSpec((None, d, tn), lambda l, j: (l, 0, jnp.minimum(j, n_tail - 1))),
            pl.BlockSpec((None, d, tn), lambda l, j: (l, 0, jnp.maximum(j - n_tail, 0))),
            pl.BlockSpec((None, d, GATE_COLS), lambda l, j: (l, 0, 0)),
        ],
        out_shape=[jax.ShapeDtypeStruct((n_layers, d, n_tail * tn), BF16),
                   jax.ShapeDtypeStruct((n_layers, d, gate_lo), BF16),
                   jax.ShapeDtypeStruct((n_layers, d, GATE_COLS), BF16)],
        compiler_params=_cparams(2),
        name="cast_w_in",
    )(w_in, w_in)


def _inproj_kernel(x_ref, g_ref, wt_ref, wh_ref, wg_ref, z_ref, kv_ref, zg_ref, h_ref, *, n_tail, j_k):
    j = pl.program_id(1)

    @pl.when(j == 0)
    def _():
        h = _rms(x_ref[...], g_ref[...]).astype(BF16)
        h_ref[...] = h
        zg_ref[...] = jnp.dot(h, wg_ref[...], preferred_element_type=F32)

    @pl.when(j < n_tail)
    def _():
        z_ref[...] = jnp.dot(h_ref[...], wt_ref[...], preferred_element_type=F32).astype(z_ref.dtype)

    @pl.when(j >= n_tail)
    def _():
        r = jnp.dot(h_ref[...], wh_ref[...], preferred_element_type=F32)
        z_ref[...] = r.astype(z_ref.dtype)

        @pl.when((j == j_k) | (j == j_k + 1))
        def _():
            kv_ref[...] = r


def in_proj(x, g, w_tail, w_head, w_gate, layer, tm, tn):
    m, d = x.shape
    n_tail = w_tail.shape[-1] // tn
    assert tn == WA and COL_VA == COL_KA + WA
    j_k = COL_KA // tn
    return pl.pallas_call(
        functools.partial(_inproj_kernel, n_tail=n_tail, j_k=j_k),
        grid=(m // tm, Z_COLS // tn),
        in_specs=[
            pl.BlockSpec((tm, d), lambda i, j: (i, 0)),
            pl.BlockSpec((None, 1, d), lambda i, j: (layer, 0, 0)),
            pl.BlockSpec((None, d, tn), lambda i, j: (layer, 0, jnp.minimum(j, n_tail - 1))),
            pl.BlockSpec((None, d, tn), lambda i, j: (layer, 0, jnp.maximum(j - n_tail, 0))),
            pl.BlockSpec((None, d, GATE_COLS), lambda i, j: (layer, 0, 0)),
        ],
        out_specs=[
            pl.BlockSpec((tm, tn), lambda i, j: (i, j)),
            pl.BlockSpec((tm, WA), lambda i, j: (i, jnp.clip(j - j_k, 0, 1))),
            pl.BlockSpec((tm, GATE_COLS), lambda i, j: (i, 0)),
        ],
        out_shape=[jax.ShapeDtypeStruct((m, Z_COLS), BF16), jax.ShapeDtypeStruct((m, 2 * WA), F32),
                   jax.ShapeDtypeStruct((m, GATE_COLS), F32)],
        scratch_shapes=[pltpu.VMEM((tm, d), BF16)],
        compiler_params=_cparams(2),
        name="in_proj",
    )(x, g, w_tail, w_head, w_gate)


def _kv_layout_kernel(*refs, tm):
    kv_refs, (ko_ref, vo_ref) = refs[:-2], refs[-2:]
    layer = pl.program_id(0)
    for a, kv_ref in enumerate(kv_refs):
        @pl.when(layer == a)
        def _(kv_ref=kv_ref):
            for h in range(HA):
                ko_ref[pl.ds(h, tm, stride=HA), :] = kv_ref[:, h * DV_A:(h + 1) * DV_A]
                vo_ref[pl.ds(h, tm, stride=HA), :] = kv_ref[:, WA + h * DV_A:WA + (h + 1) * DV_A]


def kv_layout(kvs, tm):
    n_layers = len(kvs)
    m = kvs[0].shape[0]
    n_i = m // tm

    def kv_spec(a):
        return pl.BlockSpec((tm, 2 * WA), lambda l, i: (jnp.where(l == a, i, jnp.where(l < a, 0, n_i - 1)), 0))

    out = jax.ShapeDtypeStruct((n_layers, m * HA, DV_A), F32)
    return pl.pallas_call(
        functools.partial(_kv_layout_kernel, tm=tm),
        grid=(n_layers, n_i),
        in_specs=[kv_spec(a) for a in range(n_layers)],
        out_specs=[pl.BlockSpec((None, tm * HA, DV_A), lambda l, i: (l, i, 0))] * 2,
        out_shape=[out, out],
        compiler_params=_cparams(2),
        name="kv_layout",
    )(*kvs)


def _attn_prompt_kernel(slope_ref, lam_ref, q_ref, k_ref, v_ref, qfeat_ref, kfeat_ref, g_ref, o_ref,
                        kb_ref, vt_ref, s_ref, m_ref, acc_ref, *, tq, tk, out_scale):
    h = pl.program_id(1)
    qi = pl.program_id(2)
    n_blocks = vt_ref.shape[0]
    n_sub = tq // tk

    @pl.when(qi == 0)
    def _():
        kb_ref[:, :DV_A] = k_ref[...].astype(BF16)
        kb_ref[:, DV_A:] = kfeat_ref[...]
        for jb in range(n_blocks):
            vt_ref[jb, :DV_A, :] = v_ref[jb * tk:(jb + 1) * tk, :].astype(F32).T.astype(BF16)
            vt_ref[jb, DV_A:, :] = jnp.ones((vt_ref.shape[1] - DV_A, tk), BF16)

    slope = slope_ref[h]
    lam = lam_ref[0]
    q = q_ref[...] * jnp.asarray(QK_HALF ** -0.5, q_ref.dtype)
    lane = lax.broadcasted_iota(jnp.int32, (tq, DV_A), 1)
    qs = jnp.concatenate([jnp.where(lane < QK_HALF, q, 0), jnp.where(lane >= QK_HALF, q, 0)], axis=0)
    qaug = jnp.concatenate([qs.astype(BF16), qfeat_ref[...]], axis=1)

    def segments(u):
        return [(c * tq + u * tk, tq - u * tk) for c in range(2)]

    def scores(jb, segs=((0, 2 * tq),)):
        kj = kb_ref[pl.ds(pl.multiple_of(jb * tk, tk), tk), :]
        qa = jnp.concatenate([qaug[a:a + w] for a, w in segs], axis=0)
        return lax.dot_general(kj, qa, (((1,), (1,)), ((), ())), preferred_element_type=F32)

    def softmax_pv(jb, s, off, segs=((0, 2 * tq),)):
        m = jnp.concatenate([m_ref[:, a:a + w] for a, w in segs], axis=1)
        m_new = jnp.maximum(m, jnp.max(s, axis=0, keepdims=True) - off)
        alpha = jnp.exp(m - m_new)
        p = jnp.exp(s - (m_new + off))
        pv = jnp.dot(vt_ref[jb], p.astype(BF16), preferred_element_type=F32)
        pos = 0
        for a, w in segs:
            acc_ref[:, a:a + w] = alpha[:, pos:pos + w] * acc_ref[:, a:a + w] + pv[:, pos:pos + w]
            m_ref[:, a:a + w] = m_new[:, pos:pos + w]
            pos += w

    def causal(s, width):
        krow = lax.broadcasted_iota(jnp.int32, s.shape, 0)
        qcol = jnp.bitwise_and(lax.broadcasted_iota(jnp.int32, s.shape, 1), width - 1)
        return jnp.where(krow <= qcol, s, NEG)

    def full_blocks(jj, carry):
        for u in range(n_sub):
            jb = jj * n_sub + u
            s_ref[(u + 1) % 2] = scores(jb + 1)
            softmax_pv(jb, s_ref[u % 2], slope * (qi * tq - jb * tk).astype(F32))
        return carry

    s_ref[0] = scores(0)
    m_ref[...] = jnp.full(m_ref.shape, NEG, F32)
    acc_ref[...] = jnp.zeros(acc_ref.shape, F32)
    lax.fori_loop(0, qi, full_blocks, 0)
    for u in range(n_sub):
        jb = qi * n_sub + u
        width = tq - u * tk
        if u + 1 < n_sub:
            s_ref[(u + 1) % 2, :, :2 * (width - tk)] = scores(jb + 1, segments(u + 1))
        softmax_pv(jb, causal(s_ref[u % 2, :, :2 * width], width), slope * (-u * tk), segments(u))
    o = acc_ref[:DV_A, :] / acc_ref[DV_A:DV_A + 1, :]
    a = o[:, :tq] - lam * o[:, tq:]
    y = a * lax.rsqrt(jnp.mean(a * a, axis=0, keepdims=True) + EPS) * g_ref[...] * out_scale
    o_ref[...] = y.T.astype(o_ref.dtype)


def attn_prompt(z3, slopes, lam, subln_col, layer, tq, tk):
    b, seq, _ = z3.shape
    assert tk <= 256 and tq == 2 * tk and tq & (tq - 1) == 0 and tk & (tk - 1) == 0
    lam_init = 0.8 - 0.6 * math.exp(-0.3 * layer)
    kern = functools.partial(_attn_prompt_kernel, tq=tq, tk=tk, out_scale=1.0 - lam_init)
    slopes_np = 2.0 ** (-8.0 * np.arange(1, HA + 1) / HA)
    assert np.all(np.log2(slopes_np) == np.round(np.log2(slopes_np)))
    qpos = np.arange(2 * tq) % tq
    qfeat = np.zeros((HA, 2 * tq, DV_A), np.float32)
    qfeat[:, :, 0] = slopes_np[:, None]
    qfeat[:, :, 1] = -slopes_np[:, None] * (qpos % 256)[None, :]
    qfeat[:, :, 2] = -slopes_np[:, None] * (qpos - qpos % 256)[None, :]
    kfeat = np.zeros((seq, DV_A), np.float32)
    kfeat[:, 0] = np.arange(seq) % tk
    kfeat[:, 1:3] = 1.0
    return pl.pallas_call(
        kern,
        grid=(b, HA, seq // tq),
        in_specs=[
            pl.BlockSpec(memory_space=pltpu.SMEM),
            pl.BlockSpec(memory_space=pltpu.SMEM),
            pl.BlockSpec((None, tq, DV_A), lambda bi, h, qi: (bi, qi, COL_QA // DV_A + h)),
            pl.BlockSpec((None, seq, DV_A), lambda bi, h, qi: (bi, 0, COL_KA // DV_A + h)),
            pl.BlockSpec((None, seq, DV_A), lambda bi, h, qi: (bi, 0, COL_VA // DV_A + h)),
            pl.BlockSpec((None, 2 * tq, DV_A), lambda bi, h, qi: (h, 0, 0)),
            pl.BlockSpec((seq, DV_A), lambda bi, h, qi: (0, 0)),
            pl.BlockSpec((None, DV_A, 1), lambda bi, h, qi: (layer, 0, 0)),
        ],
        out_specs=pl.BlockSpec((None, tq, DV_A), lambda bi, h, qi: (bi, qi, h)),
        out_shape=jax.ShapeDtypeStruct((b, seq, WA), BF16),
        scratch_shapes=[pltpu.VMEM((seq, 2 * DV_A), BF16), pltpu.VMEM((seq // tk, DV_A + 16, tk), BF16),
                        pltpu.VMEM((2, tk, 2 * tq), F32), pltpu.VMEM((1, 2 * tq), F32),
                        pltpu.VMEM((DV_A + 16, 2 * tq), F32)],
        compiler_params=_cparams(3),
        name="attn_prompt",
    )(slopes, lam, z3, z3, z3, jnp.asarray(qfeat, BF16), jnp.asarray(kfeat, BF16), subln_col)


def _attn_decode_kernel(pt_ref, lam_ref, qm_ref, kown_ref, vown_ref, bias_ref, slope_ref, g_ref, *rest,
                        pages_per_step, past_len, out_scale):
    del pt_ref
    k_refs = rest[:pages_per_step]
    v_refs = rest[pages_per_step:2 * pages_per_step]
    o_ref, m_ref, l_ref, acc_ref = rest[2 * pages_per_step:]
    i = pl.program_id(1)
    qm = qm_ref[...]

    @pl.when(i == 0)
    def _():
        ko = kown_ref[...].astype(BF16).astype(F32)
        m_ref[...] = jnp.sum(qm.astype(F32) * ko, axis=-1, keepdims=True)
        l_ref[...] = jnp.ones_like(l_ref)
        acc_ref[...] = vown_ref[...].astype(BF16).astype(F32)

    ss = []
    for gi in range(pages_per_step):
        kp = k_refs[gi][...].astype(BF16)
        s = lax.dot_general(qm, kp, (((1,), (1,)), ((), ())), preferred_element_type=F32)
        page_pos = i * pages_per_step + gi
        off = (page_pos * PAGE_SIZE - past_len).astype(F32)
        ss.append(s + (bias_ref[...] + slope_ref[...] * off))
    s = jnp.concatenate(ss, axis=-1)
    m_old = m_ref[...]
    m_new = jnp.maximum(m_old, jnp.max(s, axis=-1, keepdims=True))
    alpha = jnp.exp(m_old - m_new)
    p = jnp.exp(s - m_new)
    l_ref[...] = alpha * l_ref[...] + jnp.sum(p, axis=-1, keepdims=True)
    pv = jnp.zeros(acc_ref.shape, F32)
    n_rows = PAGE_SIZE * HA
    for gi in range(pages_per_step):
        vp = v_refs[gi][...].astype(BF16)
        pv = pv + jnp.dot(p[:, gi * n_rows:(gi + 1) * n_rows].astype(BF16), vp, preferred_element_type=F32)
    acc_ref[...] = alpha * acc_ref[...] + pv
    m_ref[...] = m_new

    @pl.when(i == pl.num_programs(1) - 1)
    def _():
        o = acc_ref[...] / l_ref[...]
        a = o[:HA] - lam_ref[0] * o[HA:]
        o_ref[...] = (_rms(a, g_ref[...]) * out_scale).astype(o_ref.dtype)


def attn_decode(qm, kown, vown, bias_tile, slope_col, lam, subln_g, cache_k4, cache_v4, page_table_flat, layer,
                pages_per_step):
    bs = qm.shape[0]
    n_pages = page_table_flat.shape[0] // bs
    n_rows = PAGE_SIZE * HA
    lam_init = 0.8 - 0.6 * math.exp(-0.3 * layer)
    kern = functools.partial(_attn_decode_kernel, pages_per_step=pages_per_step,
                             past_len=n_pages * PAGE_SIZE, out_scale=1.0 - lam_init)

    def page_spec(gi):
        return pl.BlockSpec((None, None, n_rows, DV_A),
                            lambda b, i, pt: (layer, pt[b * n_pages + i * pages_per_step + gi], 0, 0))

    grid_spec = pltpu.PrefetchScalarGridSpec(
        num_scalar_prefetch=1,
        grid=(bs, n_pages // pages_per_step),
        in_specs=[
            pl.BlockSpec(memory_space=pltpu.SMEM),
            pl.BlockSpec((None, 2 * HA, DV_A), lambda b, i, pt: (b, 0, 0)),
            pl.BlockSpec((None, 2 * HA, DV_A), lambda b, i, pt: (b, 0, 0)),
            pl.BlockSpec((None, 2 * HA, DV_A), lambda b, i, pt: (b, 0, 0)),
            pl.BlockSpec((2 * HA, n_rows), lambda b, i, pt: (0, 0)),
            pl.BlockSpec((2 * HA, 1), lambda b, i, pt: (0, 0)),
            pl.BlockSpec((None, 1, DV_A), lambda b, i, pt: (layer, 0, 0)),
        ] + [page_spec(gi) for gi in range(pages_per_step)] * 2,
        out_specs=pl.BlockSpec((None, HA, DV_A), lambda b, i, pt: (b, 0, 0)),
        scratch_shapes=[pltpu.VMEM((2 * HA, 1), F32), pltpu.VMEM((2 * HA, 1), F32),
                        pltpu.VMEM((2 * HA, DV_A), F32)],
    )
    return pl.pallas_call(
        kern,
        grid_spec=grid_spec,
        out_shape=jax.ShapeDtypeStruct((bs, HA, DV_A), BF16),
        compiler_params=_cparams(2),
        name="attn_decode",
    )(page_table_flat, lam, qm, kown, vown, bias_tile, slope_col, subln_g,
      *([cache_k4] * pages_per_step), *([cache_v4] * pages_per_step))


def _log_sigmoid(x):
    return jnp.minimum(x, 0.0) - jnp.log1p(jnp.exp(-jnp.abs(x)))


def _mlstm_prompt_kernel(bi_ref, bf_ref, q_ref, k_ref, v_ref, ob_ref, zg_ref, g_ref,
                         hb_ref, c_out_ref, n_out_ref, m_out_ref, c_ref, n_ref, m_ref, *, chunk):
    ci = pl.program_id(1)

    @pl.when(ci == 0)
    def _():
        c_ref[...] = jnp.zeros_like(c_ref)
        n_ref[...] = jnp.zeros_like(n_ref)
        m_ref[...] = jnp.zeros_like(m_ref)

    lane_g = lax.broadcasted_iota(jnp.int32, (1, GATE_COLS), 1)
    gate_bias = jnp.zeros((1, GATE_COLS), F32)
    for h in range(HB):
        gate_bias = jnp.where(lane_g == h, bi_ref[h], jnp.where(lane_g == HB + h, bf_ref[h], gate_bias))
    pre = zg_ref[...] + gate_bias
    gates = jnp.where(lane_g < HB, pre, _log_sigmoid(pre))
    gates_t = gates.T
    row = lax.broadcasted_iota(jnp.int32, (chunk, chunk), 0)
    col = lax.broadcasted_iota(jnp.int32, (chunk, chunk), 1)
    causal = col <= row
    g_norm = g_ref[...]
    q_scale = jnp.asarray(DK ** -0.5, q_ref.dtype)
    heads = range(HB)
    sls = [slice(h * DK, (h + 1) * DK) for h in heads]
    qs = [(q_ref[:, sl] * q_scale).astype(BF16) for sl in sls]
    kb = [k_ref[:, sl].astype(BF16) for sl in sls]
    vb = [v_ref[:, sl].astype(BF16) for sl in sls]
    c_old = [c_ref[h] for h in heads]
    n_old = [n_ref[h:h + 1, :] for h in heads]
    s_raw = [lax.dot_general(qs[h], kb[h], (((1,), (1,)), ((), ())), preferred_element_type=F32) for h in heads]
    inter = [jnp.dot(qs[h], c_old[h].astype(BF16), preferred_element_type=F32) for h in heads]

    b_col, m_t, w_inter, wmat = [], [], [], []
    for h in heads:
        i_row = gates_t[h:h + 1, :]
        lf_col = gates[:, HB + h:HB + h + 1]
        lf_row = gates_t[HB + h:HB + h + 1, :]
        bc = jnp.sum(jnp.where(causal, lf_row, 0.0), axis=1, keepdims=True)
        br = jnp.sum(jnp.where(row <= col, lf_col, 0.0), axis=0, keepdims=True)
        g_col = bc + m_ref[h:h + 1, 0:1]
        dmat = jnp.where(causal, bc - br + i_row, NEG)
        mt = jnp.maximum(g_col, jnp.max(dmat, axis=1, keepdims=True))
        b_col.append(bc)
        m_t.append(mt)
        w_inter.append(jnp.exp(g_col - mt))
        wmat.append(jnp.exp(dmat - mt))

    s = [s_raw[h] * wmat[h] for h in heads]
    sv = [jnp.dot(s[h].astype(BF16), vb[h], preferred_element_type=F32) for h in heads]

    kw = []
    for h in heads:
        m_last = m_t[h][chunk - 1:chunk, :]
        wc_col = jnp.exp(b_col[h][chunk - 1:chunk, :] - b_col[h] + gates[:, h:h + 1] - m_last)
        kw.append(kb[h].astype(F32) * wc_col.astype(BF16).astype(F32))
    kv_new = [jnp.dot(kw[h].T.astype(BF16), vb[h], preferred_element_type=F32) for h in heads]

    for h in heads:
        num = w_inter[h] * inter[h] + sv[h]
        qn = jnp.sum(qs[h].astype(F32) * n_old[h].astype(BF16).astype(F32), axis=-1, keepdims=True)
        den = w_inter[h] * qn + jnp.sum(s[h], axis=-1, keepdims=True)
        hv = num / jnp.maximum(jnp.abs(den), jnp.exp(-m_t[h]))
        hb = _rms(hv, g_norm) * jax.nn.sigmoid(ob_ref[:, sls[h]].astype(F32))
        hb_ref[:, sls[h]] = hb.astype(hb_ref.dtype)
        w_last = w_inter[h][chunk - 1:chunk, :]
        c_ref[h] = w_last * c_old[h] + kv_new[h]
        n_ref[h:h + 1, :] = w_last * n_old[h] + jnp.sum(kw[h], axis=0, keepdims=True)
        m_ref[h:h + 1, :] = jnp.broadcast_to(m_t[h][chunk - 1:chunk, :], (1, m_ref.shape[1]))

    @pl.when(ci == pl.num_programs(1) - 1)
    def _():
        c_out_ref[...] = c_ref[...]
        n_out_ref[...] = n_ref[...]
        m_out_ref[...] = m_ref[...]


def mlstm_prompt(z3, zg3, b_i, b_f, norm_g, layer, chunk):
    b, seq, _ = z3.shape

    def zspec(col):
        return pl.BlockSpec((None, chunk, WB), lambda bi, ci: (bi, ci, col // WB))

    return pl.pallas_call(
        functools.partial(_mlstm_prompt_kernel, chunk=chunk),
        grid=(b, seq // chunk),
        in_specs=[
            pl.BlockSpec(memory_space=pltpu.SMEM),
            pl.BlockSpec(memory_space=pltpu.SMEM),
            zspec(COL_QB), zspec(COL_KB), zspec(COL_VB), zspec(COL_OB),
            pl.BlockSpec((None, chunk, GATE_COLS), lambda bi, ci: (bi, ci, 0)),
            pl.BlockSpec((None, 1, DV_B), lambda bi, ci: (layer, 0, 0)),
        ],
        out_specs=[
            pl.BlockSpec((None, chunk, WB), lambda bi, ci: (bi, ci, 0)),
            pl.BlockSpec((None, HB, DK, DV_B), lambda bi, ci: (bi, 0, 0, 0)),
            pl.BlockSpec((None, HB, DK), lambda bi, ci: (bi, 0, 0)),
            pl.BlockSpec((None, HB, 128), lambda bi, ci: (bi, 0, 0)),
        ],
        out_shape=[
            jax.ShapeDtypeStruct((b, seq, WB), BF16),
            jax.ShapeDtypeStruct((b, HB, DK, DV_B), F32),
            jax.ShapeDtypeStruct((b, HB, DK), F32),
            jax.ShapeDtypeStruct((b, HB, 128), F32),
        ],
        scratch_shapes=[pltpu.VMEM((HB, DK, DV_B), F32), pltpu.VMEM((HB, DK), F32), pltpu.VMEM((HB, 128), F32)],
        compiler_params=_cparams(2),
        name="mlstm_prompt",
    )(b_i, b_f, z3, z3, z3, z3, zg3, norm_g)


def _mlstm_step_kernel(bi_ref, bf_ref, q_ref, kcol_ref, krow_ref, v_ref, ob_ref, zg_ref, g_ref,
                       c0_ref, n0_ref, m0_ref, hb_ref, c_out_ref, n_out_ref, m_out_ref):
    gates = zg_ref[...]
    g_norm = g_ref[...]
    for h in range(HB):
        sl = slice(h * DK, (h + 1) * DK)
        i_pre = gates[:, h:h + 1] + bi_ref[h]
        lf = _log_sigmoid(gates[:, HB + h:HB + h + 1] + bf_ref[h])
        m0 = m0_ref[h:h + 1, 0:1]
        g_st = lf + m0
        m_t = jnp.maximum(g_st, i_pre)
        w_inter = jnp.exp(g_st - m_t)
        w_in = jnp.exp(i_pre - m_t)
        qs = (q_ref[:, sl] * (DK ** -0.5)).astype(BF16)
        k_row = krow_ref[:, sl].astype(BF16).astype(F32)
        k_col = kcol_ref[h].astype(BF16).astype(F32)
        v_row = v_ref[:, sl].astype(BF16).astype(F32)
        c_old = c0_ref[h]
        n_old = n0_ref[h:h + 1, :]
        qf = qs.astype(F32)
        s = jnp.sum(qf * k_row, axis=-1, keepdims=True) * w_in
        q16 = jnp.broadcast_to(qs, (16, DK))
        qc = jnp.dot(q16, c_old.astype(BF16), preferred_element_type=F32)[0:1, :]
        num = w_inter * qc + s.astype(BF16).astype(F32) * v_row
        qn = jnp.sum(qf * n_old.astype(BF16).astype(F32), axis=-1, keepdims=True)
        den = w_inter * qn + s
        hv = num / jnp.maximum(jnp.abs(den), jnp.exp(-m_t))
        hb = _rms(hv, g_norm) * jax.nn.sigmoid(ob_ref[:, sl])
        hb_ref[:, sl] = hb.astype(hb_ref.dtype)
        wb = w_in.astype(BF16).astype(F32)
        kw_col = (k_col * wb).astype(BF16).astype(F32)
        c_out_ref[h] = w_inter * c_old + kw_col * v_row
        n_out_ref[h:h + 1, :] = w_inter * n_old + wb * k_row
        m_out_ref[h:h + 1, :] = jnp.broadcast_to(m_t, (1, m_out_ref.shape[1]))


def mlstm_step(zs3, kcol, zgs3, b_i, b_f, norm_g, state_c, state_n, m0_lanes, layer):
    bs = zs3.shape[0]

    def zspec(col):
        return pl.BlockSpec((None, 1, WB), lambda bi: (bi, 0, col // WB))

    return pl.pallas_call(
        _mlstm_step_kernel,
        grid=(bs,),
        in_specs=[
            pl.BlockSpec(memory_space=pltpu.SMEM),
            pl.BlockSpec(memory_space=pltpu.SMEM),
            zspec(COL_QB),
            pl.BlockSpec((None, HB, DK, 1), lambda bi: (bi, 0, 0, 0)),
            zspec(COL_KB), zspec(COL_VB), zspec(COL_OB),
            pl.BlockSpec((None, 1, GATE_COLS), lambda bi: (bi, 0, 0)),
            pl.BlockSpec((None, 1, DV_B), lambda bi: (layer, 0, 0)),
            pl.BlockSpec((None, None, HB, DK, DV_B), lambda bi: (layer, bi, 0, 0, 0)),
            pl.BlockSpec((None, None, HB, DK), lambda bi: (layer, bi, 0, 0)),
            pl.BlockSpec((None, None, HB, 128), lambda bi: (layer, bi, 0, 0)),
        ],
        out_specs=[
            pl.BlockSpec((None, 1, WB), lambda bi: (bi, 0, 0)),
            pl.BlockSpec((None, HB, DK, DV_B), lambda bi: (bi, 0, 0, 0)),
            pl.BlockSpec((None, HB, DK), lambda bi: (bi, 0, 0)),
            pl.BlockSpec((None, HB, 128), lambda bi: (bi, 0, 0)),
        ],
        out_shape=[
            jax.ShapeDtypeStruct((bs, 1, WB), BF16),
            jax.ShapeDtypeStruct((bs, HB, DK, DV_B), F32),
            jax.ShapeDtypeStruct((bs, HB, DK), F32),
            jax.ShapeDtypeStruct((bs, HB, 128), F32),
        ],
        compiler_params=_cparams(1),
        name="mlstm_step",
    )(b_i, b_f, zs3, kcol, zs3, zs3, zs3, zgs3, norm_g, state_c, state_n, m0_lanes)


def _merge_out_kernel(a_ref, hb_ref, ga_ref, gb_ref, x_ref, wa_ref, wb_ref, wo_ref, o_ref):
    pa = jnp.dot(a_ref[...], wa_ref[...], preferred_element_type=F32)
    pb = jnp.dot(hb_ref[...], wb_ref[...], preferred_element_type=F32)
    merged = jax.nn.sigmoid(ga_ref[...].astype(F32)) * pa + jax.nn.sigmoid(gb_ref[...].astype(F32)) * pb
    o_ref[...] = x_ref[...] + jnp.dot(merged.astype(BF16), wo_ref[...], preferred_element_type=F32)


def merge_out(a, hb, z, x, w_a, w_b, w_o, layer, tm):
    m, d = x.shape
    once = pl.Buffered(1)
    return pl.pallas_call(
        _merge_out_kernel,
        grid=(m // tm,),
        in_specs=[
            pl.BlockSpec((tm, WA), lambda i: (i, 0)),
            pl.BlockSpec((tm, WB), lambda i: (i, 0)),
            pl.BlockSpec((tm, d), lambda i: (i, 0)),
            pl.BlockSpec((tm, d), lambda i: (i, 1)),
            pl.BlockSpec((tm, d), lambda i: (i, 0)),
            pl.BlockSpec((None, WA, d), lambda i: (layer, 0, 0), pipeline_mode=once),
            pl.BlockSpec((None, WB, d), lambda i: (layer, 0, 0), pipeline_mode=once),
            pl.BlockSpec((None, d, d), lambda i: (layer, 0, 0), pipeline_mode=once),
        ],
        out_specs=pl.BlockSpec((tm, d), lambda i: (i, 0)),
        out_shape=jax.ShapeDtypeStruct((m, d), F32),
        compiler_params=_cparams(1),
        name="merge_out",
    )(a, hb, z, z, x, w_a, w_b, w_o)


def _ffn_kernel(x_ref, g_ref, wg_ref, wu_ref, wd_ref, gf_ref, o_ref, h_ref, *, final_norm):
    f = pl.program_id(1)

    @pl.when(f == 0)
    def _():
        x = x_ref[...]
        h_ref[...] = _rms(x, g_ref[...]).astype(BF16)
        o_ref[...] = x

    h = h_ref[...]
    gate = jnp.dot(h, wg_ref[...], preferred_element_type=F32)
    up = jnp.dot(h, wu_ref[...], preferred_element_type=F32)
    act = (gate * jax.nn.sigmoid(gate) * up).astype(BF16)
    o_ref[...] += jnp.dot(act, wd_ref[...], preferred_element_type=F32)

    if final_norm:
        @pl.when(f == pl.num_programs(1) - 1)
        def _():
            o_ref[...] = _rms(o_ref[...], gf_ref[...])


def ffn(x, g, w_gu, w_d, final_g, layer, tm, tf, final_norm):
    m, d = x.shape
    nf = D_FF // tf
    return pl.pallas_call(
        functools.partial(_ffn_kernel, final_norm=final_norm),
        grid=(m // tm, nf),
        in_specs=[
            pl.BlockSpec((tm, d), lambda i, f: (i, 0)),
            pl.BlockSpec((None, 1, d), lambda i, f: (layer, 0, 0)),
            pl.BlockSpec((None, d, tf), lambda i, f: (layer, 0, f)),
            pl.BlockSpec((None, d, tf), lambda i, f: (layer, 0, nf + f)),
            pl.BlockSpec((None, tf, d), lambda i, f: (layer, f, 0)),
            pl.BlockSpec((1, d), lambda i, f: (0, 0)),
        ],
        out_specs=pl.BlockSpec((tm, d), lambda i, f: (i, 0)),
        out_shape=jax.ShapeDtypeStruct((m, d), F32),
        scratch_shapes=[pltpu.VMEM((tm, d), BF16)],
        compiler_params=_cparams(2),
        name="ffn",
    )(x, g, w_gu, w_gu, w_d, final_g)


def kernel(x_prompt, x_sample, cache_k, cache_v, state_C, state_n, state_m, page_table, norm1_g, w_in, b_igate,
           b_fgate, lam_q1, lam_k1, lam_q2, lam_k2, subln_g, mlstm_norm_g, w_proj_a, w_proj_b, w_out, norm2_g,
           w_gu, w_down, final_g):
    bp, lp, d = x_prompt.shape
    bs = x_sample.shape[0]
    n_pool = cache_k.shape[1]
    n_pages = page_table.shape[1]
    past_len = n_pages * PAGE_SIZE
    n_rows = PAGE_SIZE * HA

    w_tail, w_head, w_gate = cast_w_in(w_in, tn=WA)
    w_a = w_proj_a.astype(BF16)
    w_b = w_proj_b.astype(BF16)
    w_o = w_out.astype(BF16)
    w_gu_b = w_gu.astype(BF16)
    w_d = w_down.astype(BF16)
    norm1 = norm1_g.reshape(DEPTH, 1, d)
    norm2 = norm2_g.reshape(DEPTH, 1, d)
    final_row = final_g.reshape(1, d)
    subln = subln_g.reshape(DEPTH, 1, DV_A)
    subln_col = subln_g.reshape(DEPTH, DV_A, 1)
    mnorm = mlstm_norm_g.reshape(DEPTH, 1, DV_B)

    slopes = jnp.asarray(2.0 ** (-8.0 * np.arange(1, HA + 1) / HA), dtype=F32)
    lam_all = (jnp.exp(jnp.sum(lam_q1 * lam_k1, axis=-1)) - jnp.exp(jnp.sum(lam_q2 * lam_k2, axis=-1))
               + jnp.asarray([0.8 - 0.6 * math.exp(-0.3 * l) for l in range(DEPTH)], dtype=F32))

    key_head = np.arange(n_rows) % HA
    key_tok = np.arange(n_rows) // HA
    q_head = np.arange(2 * HA) % HA
    slopes_np = 2.0 ** (-8.0 * np.arange(1, HA + 1) / HA)
    assert np.all(np.log2(slopes_np) == np.round(np.log2(slopes_np)))
    bias_np = np.where(key_head[None, :] == q_head[:, None], slopes_np[q_head][:, None] * key_tok[None, :], NEG)
    bias_tile = jnp.asarray(bias_np, dtype=F32)
    slope_col = jnp.asarray(slopes_np[q_head][:, None], dtype=F32)
    half_mask = jnp.asarray((np.arange(DV_A)[None, None, :] // QK_HALF) == np.arange(2)[:, None, None])

    cache_k4 = cache_k.reshape(DEPTH, n_pool, n_rows, DV_A)
    cache_v4 = cache_v.reshape(DEPTH, n_pool, n_rows, DV_A)
    pt_flat = page_table.reshape(-1)
    m0_lanes = jnp.broadcast_to(state_m[..., None], state_m.shape + (128,))

    xp = x_prompt.reshape(bp * lp, d)
    xs = x_sample.reshape(bs, d)
    outs = {k: [] for k in ("cp", "np", "mp", "ks", "vs", "cs", "ns", "ms")}
    kv_layers = []
    for l in range(DEPTH):
        lam = lam_all[l:l + 1]
        z, kv, zg = in_proj(xp, norm1, w_tail, w_head, w_gate, l, tm=1024, tn=1024)
        z3 = z.reshape(bp, lp, Z_COLS)
        kv_layers.append(kv)
        a = attn_prompt(z3, slopes, lam, subln_col, l, tq=512, tk=256)
        hb, c_p, n_p, m_p = mlstm_prompt(z3, zg.reshape(bp, lp, GATE_COLS), b_igate[l], b_fgate[l], mnorm, l,
                                         chunk=256)
        outs["cp"].append(c_p)
        outs["np"].append(n_p)
        outs["mp"].append(m_p[:, :, 0])
        xp = merge_out(a.reshape(bp * lp, WA), hb.reshape(bp * lp, WB), z, xp, w_a, w_b, w_o, l, tm=256)
        xp = ffn(xp, norm2, w_gu_b, w_d, final_row, l, tm=1024, tf=512, final_norm=l == DEPTH - 1)

        zs_b, kvs, zgs = in_proj(xs, norm1, w_tail, w_head, w_gate, l, tm=bs, tn=1024)
        zs = zs_b.astype(F32)
        k_new = kvs[:, :WA].reshape(bs, HA, DV_A)
        v_new = kvs[:, WA:].reshape(bs, HA, DV_A)
        outs["ks"].append(k_new.reshape(bs, 1, HA, DV_A))
        outs["vs"].append(v_new.reshape(bs, 1, HA, DV_A))
        q_new = zs[:, COL_QA:COL_QA + WA].reshape(bs, 1, HA, DV_A) * (QK_HALF ** -0.5)
        qm = jnp.where(half_mask[None], q_new, 0.0).reshape(bs, 2 * HA, DV_A).astype(BF16)
        kown = jnp.tile(k_new, (1, 2, 1))
        vown = jnp.tile(v_new, (1, 2, 1))
        a_s = attn_decode(qm, kown, vown, bias_tile, slope_col, lam, subln, cache_k4, cache_v4, pt_flat, l,
                          pages_per_step=8)
        zs3 = zs.reshape(bs, 1, Z_COLS)
        kcol = zs[:, COL_KB:COL_KB + WB].reshape(bs, HB, DK, 1)
        hb_s, c_s, n_s, m_s = mlstm_step(zs3, kcol, zgs.reshape(bs, 1, GATE_COLS), b_igate[l], b_fgate[l], mnorm,
                                         state_C, state_n, m0_lanes, l)
        outs["cs"].append(c_s)
        outs["ns"].append(n_s)
        outs["ms"].append(m_s[:, :, 0])
        xs = merge_out(a_s.reshape(bs, WA), hb_s.reshape(bs, WB), zs_b, xs, w_a, w_b, w_o, l, tm=bs)
        xs = ffn(xs, norm2, w_gu_b, w_d, final_row, l, tm=bs, tf=512, final_norm=l == DEPTH - 1)

    y_prompt = xp.reshape(bp, lp, d)
    y_sample = xs.reshape(bs, 1, d)
    st = {k: jnp.stack(v) for k, v in outs.items()}
    k_all, v_all = kv_layout(kv_layers, tm=256)
    st["kp"] = k_all.reshape(DEPTH, bp, lp, HA, DV_A)
    st["vp"] = v_all.reshape(DEPTH, bp, lp, HA, DV_A)
    return (y_prompt, y_sample, st["kp"], st["vp"], st["cp"], st["np"], st["mp"],
            st["ks"], st["vs"], st["cs"], st["ns"], st["ms"])
```

```python
import functools
import math

import numpy as np
import jax
import jax.numpy as jnp
from jax import lax
from jax.experimental import pallas as pl
from jax.experimental.pallas import tpu as pltpu

F32 = jnp.float32
BF16 = jnp.bfloat16

D_MODEL = 2048
DEPTH = 4
PAGE_SIZE = 128
WA = D_MODEL // 2
DV_A = 128
HA = WA // DV_A
QK_HALF = DV_A // 2
WB = D_MODEL // 2
HB = 4
DK = WB // HB
DV_B = WB // HB
D_FF = 5632
EPS = 1e-6
NEG = -1e30

Z_COLS = 2 * D_MODEL + 3 * WA + 4 * WB
COL_QA = 2 * D_MODEL
COL_KA = COL_QA + WA
COL_VA = COL_KA + WA
COL_QB = COL_VA + WA
COL_KB = COL_QB + WB
COL_VB = COL_KB + WB
COL_OB = COL_VB + WB
GATE_COLS = 128

VMEM_LIMIT = 56 * 1024 * 1024


def _cparams(n_axes, vmem=VMEM_LIMIT):
    return pltpu.CompilerParams(dimension_semantics=("arbitrary",) * n_axes, vmem_limit_bytes=vmem)


def _rms(x, g):
    return x * lax.rsqrt(jnp.mean(x * x, axis=-1, keepdims=True) + EPS) * g


def _cast_w_in_kernel(a_ref, b_ref, wt_ref, wh_ref, wg_ref, *, n_tail, shift):
    j = pl.program_id(1)

    @pl.when(j < n_tail)
    def _():
        wt_ref[...] = jnp.concatenate([a_ref[shift:, :], b_ref[...]], axis=0).astype(BF16)

    @pl.when(j == 0)
    def _():
        pad = jnp.zeros((wg_ref.shape[0] - shift, a_ref.shape[1]), F32)
        wg_ref[...] = jnp.concatenate([a_ref[:shift, :], pad], axis=0).astype(BF16)

    @pl.when(j >= n_tail)
    def _():
        wh_ref[...] = a_ref[...].astype(BF16)


def cast_w_in(w_in_t, tn):
    n_layers, n_in, d = w_in_t.shape
    gate_lo = 3 * WA + 4 * WB
    shift = 2 * HB
    n_head, n_tail = gate_lo // tn, (n_in - gate_lo - shift) // tn
    assert gate_lo % tn == 0 and n_in == gate_lo + shift + n_tail * tn and tn % shift == 0 and shift == 8
    return pl.pallas_call(
        functools.partial(_cast_w_in_kernel, n_tail=n_tail, shift=shift),
        grid=(n_layers, n_tail + n_head),
        in_specs=[
            pl.BlockSpec((None, tn, d), lambda l, j: (l, jnp.where(j < n_tail, n_head + j, j - n_tail), 0)),
            pl.BlockSpec((None, shift, d),
                         lambda l, j: (l, jnp.where(j < n_tail, (n_head + j + 1) * (tn // shift), 0), 0)),
        ],
        out_specs=[
            pl.BlockSpec((None, tn, d), lambda l, j: (l, jnp.minimum(j, n_tail - 1), 0)),
            pl.BlockSpec((None, tn, d), lambda l, j: (l, jnp.maximum(j - n_tail, 0), 0)),
            pl.BlockSpec((None, GATE_COLS, d), lambda l, j: (l, 0, 0)),
        ],
        out_shape=[jax.ShapeDtypeStruct((n_layers, n_tail * tn, d), BF16),
                   jax.ShapeDtypeStruct((n_layers, gate_lo, d), BF16),
                   jax.ShapeDtypeStruct((n_layers, GATE_COLS, d), BF16)],
        compiler_params=_cparams(2),
        name="cast_w_in",
    )(w_in_t, w_in_t)


def _dot_nt(a, b):
    return lax.dot_general(a, b, (((1,), (1,)), ((), ())), preferred_element_type=F32)


def _inproj_kernel(x_ref, g_ref, wt_ref, wh_ref, wg_ref, z_ref, kv_ref, zg_ref, h_ref, *, n_tail, j_k):
    j = pl.program_id(1)

    @pl.when(j == 0)
    def _():
        h = _rms(x_ref[...], g_ref[...]).astype(BF16)
        h_ref[...] = h
        zg_ref[...] = _dot_nt(h, wg_ref[...])

    @pl.when(j < n_tail)
    def _():
        z_ref[...] = _dot_nt(h_ref[...], wt_ref[...]).astype(z_ref.dtype)

    @pl.when(j >= n_tail)
    def _():
        r = _dot_nt(h_ref[...], wh_ref[...])
        z_ref[...] = r.astype(z_ref.dtype)

        @pl.when((j == j_k) | (j == j_k + 1))
        def _():
            kv_ref[...] = r


def in_proj(x, g, w_tail, w_head, w_gate, layer, tm, tn):
    m, d = x.shape
    n_tail = w_tail.shape[1] // tn
    assert tn == WA and COL_VA == COL_KA + WA
    j_k = COL_KA // tn
    return pl.pallas_call(
        functools.partial(_inproj_kernel, n_tail=n_tail, j_k=j_k),
        grid=(m // tm, Z_COLS // tn),
        in_specs=[
            pl.BlockSpec((tm, d), lambda i, j: (i, 0)),
            pl.BlockSpec((None, 1, d), lambda i, j: (layer, 0, 0)),
            pl.BlockSpec((None, tn, d), lambda i, j: (layer, jnp.minimum(j, n_tail - 1), 0)),
            pl.BlockSpec((None, tn, d), lambda i, j: (layer, jnp.maximum(j - n_tail, 0), 0)),
            pl.BlockSpec((None, GATE_COLS, d), lambda i, j: (layer, 0, 0)),
        ],
        out_specs=[
            pl.BlockSpec((tm, tn), lambda i, j: (i, j)),
            pl.BlockSpec((tm, WA), lambda i, j: (i, jnp.clip(j - j_k, 0, 1))),
            pl.BlockSpec((tm, GATE_COLS), lambda i, j: (i, 0)),
        ],
        out_shape=[jax.ShapeDtypeStruct((m, Z_COLS), BF16), jax.ShapeDtypeStruct((m, 2 * WA), F32),
                   jax.ShapeDtypeStruct((m, GATE_COLS), F32)],
        scratch_shapes=[pltpu.VMEM((tm, d), BF16)],
        compiler_params=_cparams(2),
        name="in_proj",
    )(x, g, w_tail, w_head, w_gate)


def _kv_layout_kernel(*refs, tm):
    kv_refs, (ko_ref, vo_ref) = refs[:-2], refs[-2:]
    layer = pl.program_id(0)
    for a, kv_ref in enumerate(kv_refs):
        @pl.when(layer == a)
        def _(kv_ref=kv_ref):
            for h in range(HA):
                ko_ref[pl.ds(h, tm, stride=HA), :] = kv_ref[:, h * DV_A:(h + 1) * DV_A]
                vo_ref[pl.ds(h, tm, stride=HA), :] = kv_ref[:, WA + h * DV_A:WA + (h + 1) * DV_A]


def kv_layout(kvs, tm):
    n_layers = len(kvs)
    m = kvs[0].shape[0]
    n_i = m // tm

    def kv_spec(a):
        return pl.BlockSpec((tm, 2 * WA), lambda l, i: (jnp.where(l == a, i, jnp.where(l < a, 0, n_i - 1)), 0))

    out = jax.ShapeDtypeStruct((n_layers, m * HA, DV_A), F32)
    return pl.pallas_call(
        functools.partial(_kv_layout_kernel, tm=tm),
        grid=(n_layers, n_i),
        in_specs=[kv_spec(a) for a in range(n_layers)],
        out_specs=[pl.BlockSpec((None, tm * HA, DV_A), lambda l, i: (l, i, 0))] * 2,
        out_shape=[out, out],
        compiler_params=_cparams(2),
        name="kv_layout",
    )(*kvs)


def _attn_prompt_kernel(slope_ref, lam_ref, q_ref, k_ref, v_ref, qfeat_ref, kfeat_ref, g_ref, o_ref,
                        kb_ref, vt_ref, s_ref, m_ref, acc_ref, *, tq, tk, out_scale):
    h = pl.program_id(1)
    qi = pl.program_id(2)
    n_blocks = vt_ref.shape[0]
    n_sub = tq // tk

    @pl.when(qi == 0)
    def _():
        kb_ref[:, :DV_A] = k_ref[...].astype(BF16)
        kb_ref[:, DV_A:] = kfeat_ref[...]
        for jb in range(n_blocks):
            vt_ref[jb, :DV_A, :] = v_ref[jb * tk:(jb + 1) * tk, :].astype(F32).T.astype(BF16)
            vt_ref[jb, DV_A:, :] = jnp.ones((vt_ref.shape[1] - DV_A, tk), BF16)

    slope = slope_ref[h]
    lam = lam_ref[0]
    q = q_ref[...] * jnp.asarray(QK_HALF ** -0.5, q_ref.dtype)
    lane = lax.broadcasted_iota(jnp.int32, (tq, DV_A), 1)
    qs = jnp.concatenate([jnp.where(lane < QK_HALF, q, 0), jnp.where(lane >= QK_HALF, q, 0)], axis=0)
    qaug = jnp.concatenate([qs.astype(BF16), qfeat_ref[...]], axis=1)

    def segments(u):
        return [(c * tq + u * tk, tq - u * tk) for c in range(2)]

    def scores(jb, segs=((0, 2 * tq),)):
        kj = kb_ref[pl.ds(pl.multiple_of(jb * tk, tk), tk), :]
        qa = jnp.concatenate([qaug[a:a + w] for a, w in segs], axis=0)
        return lax.dot_general(kj, qa, (((1,), (1,)), ((), ())), preferred_element_type=F32)

    def softmax_pv(jb, s, off, segs=((0, 2 * tq),)):
        m = jnp.concatenate([m_ref[:, a:a + w] for a, w in segs], axis=1)
        m_new = jnp.maximum(m, jnp.max(s, axis=0, keepdims=True) - off)
        alpha = jnp.exp(m - m_new)
        p = jnp.exp(s - (m_new + off))
        pv = jnp.dot(vt_ref[jb], p.astype(BF16), preferred_element_type=F32)
        pos = 0
        for a, w in segs:
            acc_ref[:, a:a + w] = alpha[:, pos:pos + w] * acc_ref[:, a:a + w] + pv[:, pos:pos + w]
            m_ref[:, a:a + w] = m_new[:, pos:pos + w]
            pos += w

    def causal(s, width):
        krow = lax.broadcasted_iota(jnp.int32, s.shape, 0)
        qcol = jnp.bitwise_and(lax.broadcasted_iota(jnp.int32, s.shape, 1), width - 1)
        return jnp.where(krow <= qcol, s, NEG)

    def full_blocks(jj, carry):
        for u in range(n_sub):
            jb = jj * n_sub + u
            s_ref[(u + 1) % 2] = scores(jb + 1)
            softmax_pv(jb, s_ref[u % 2], slope * (qi * tq - jb * tk).astype(F32))
        return carry

    s_ref[0] = scores(0)
    m_ref[...] = jnp.full(m_ref.shape, NEG, F32)
    acc_ref[...] = jnp.zeros(acc_ref.shape, F32)
    lax.fori_loop(0, qi, full_blocks, 0)
    for u in range(n_sub):
        jb = qi * n_sub + u
        width = tq - u * tk
        if u + 1 < n_sub:
            s_ref[(u + 1) % 2, :, :2 * (width - tk)] = scores(jb + 1, segments(u + 1))
        softmax_pv(jb, causal(s_ref[u % 2, :, :2 * width], width), slope * (-u * tk), segments(u))
    o = acc_ref[:DV_A, :] / acc_ref[DV_A:DV_A + 1, :]
    a = o[:, :tq] - lam * o[:, tq:]
    y = a * lax.rsqrt(jnp.mean(a * a, axis=0, keepdims=True) + EPS) * g_ref[...] * out_scale
    o_ref[...] = y.T.astype(o_ref.dtype)


def attn_prompt(z3, slopes, lam, subln_col, layer, tq, tk):
    b, seq, _ = z3.shape
    assert tk <= 256 and tq == 2 * tk and tq & (tq - 1) == 0 and tk & (tk - 1) == 0
    lam_init = 0.8 - 0.6 * math.exp(-0.3 * layer)
    kern = functools.partial(_attn_prompt_kernel, tq=tq, tk=tk, out_scale=1.0 - lam_init)
    slopes_np = 2.0 ** (-8.0 * np.arange(1, HA + 1) / HA)
    assert np.all(np.log2(slopes_np) == np.round(np.log2(slopes_np)))
    qpos = np.arange(2 * tq) % tq
    qfeat = np.zeros((HA, 2 * tq, DV_A), np.float32)
    qfeat[:, :, 0] = slopes_np[:, None]
    qfeat[:, :, 1] = -slopes_np[:, None] * (qpos % 256)[None, :]
    qfeat[:, :, 2] = -slopes_np[:, None] * (qpos - qpos % 256)[None, :]
    kfeat = np.zeros((seq, DV_A), np.float32)
    kfeat[:, 0] = np.arange(seq) % tk
    kfeat[:, 1:3] = 1.0
    return pl.pallas_call(
        kern,
        grid=(b, HA, seq // tq),
        in_specs=[
            pl.BlockSpec(memory_space=pltpu.SMEM),
            pl.BlockSpec(memory_space=pltpu.SMEM),
            pl.BlockSpec((None, tq, DV_A), lambda bi, h, qi: (bi, qi, COL_QA // DV_A + h)),
            pl.BlockSpec((None, seq, DV_A), lambda bi, h, qi: (bi, 0, COL_KA // DV_A + h)),
            pl.BlockSpec((None, seq, DV_A), lambda bi, h, qi: (bi, 0, COL_VA // DV_A + h)),
            pl.BlockSpec((None, 2 * tq, DV_A), lambda bi, h, qi: (h, 0, 0)),
            pl.BlockSpec((seq, DV_A), lambda bi, h, qi: (0, 0)),
            pl.BlockSpec((None, DV_A, 1), lambda bi, h, qi: (layer, 0, 0)),
        ],
        out_specs=pl.BlockSpec((None, tq, DV_A), lambda bi, h, qi: (bi, qi, h)),
        out_shape=jax.ShapeDtypeStruct((b, seq, WA), BF16),
        scratch_shapes=[pltpu.VMEM((seq, 2 * DV_A), BF16), pltpu.VMEM((seq // tk, DV_A + 16, tk), BF16),
                        pltpu.VMEM((2, tk, 2 * tq), F32), pltpu.VMEM((1, 2 * tq), F32),
                        pltpu.VMEM((DV_A + 16, 2 * tq), F32)],
        compiler_params=_cparams(3),
        name="attn_prompt",
    )(slopes, lam, z3, z3, z3, jnp.asarray(qfeat, BF16), jnp.asarray(kfeat, BF16), subln_col)


def _attn_decode_kernel(pt_ref, lam_ref, qm_ref, kown_ref, vown_ref, bias_ref, slope_ref, g_ref, *rest,
                        pages_per_step, past_len, out_scale):
    del pt_ref
    k_refs = rest[:pages_per_step]
    v_refs = rest[pages_per_step:2 * pages_per_step]
    o_ref, m_ref, l_ref, acc_ref = rest[2 * pages_per_step:]
    i = pl.program_id(1)
    qm = qm_ref[...]

    @pl.when(i == 0)
    def _():
        ko = kown_ref[...].astype(BF16).astype(F32)
        m_ref[...] = jnp.sum(qm.astype(F32) * ko, axis=-1, keepdims=True)
        l_ref[...] = jnp.ones_like(l_ref)
        acc_ref[...] = vown_ref[...].astype(BF16).astype(F32)

    ss = []
    for gi in range(pages_per_step):
        kp = k_refs[gi][...].astype(BF16)
        s = lax.dot_general(qm, kp, (((1,), (1,)), ((), ())), preferred_element_type=F32)
        page_pos = i * pages_per_step + gi
        off = (page_pos * PAGE_SIZE - past_len).astype(F32)
        ss.append(s + (bias_ref[...] + slope_ref[...] * off))
    s = jnp.concatenate(ss, axis=-1)
    m_old = m_ref[...]
    m_new = jnp.maximum(m_old, jnp.max(s, axis=-1, keepdims=True))
    alpha = jnp.exp(m_old - m_new)
    p = jnp.exp(s - m_new)
    l_ref[...] = alpha * l_ref[...] + jnp.sum(p, axis=-1, keepdims=True)
    pv = jnp.zeros(acc_ref.shape, F32)
    n_rows = PAGE_SIZE * HA
    for gi in range(pages_per_step):
        vp = v_refs[gi][...].astype(BF16)
        pv = pv + jnp.dot(p[:, gi * n_rows:(gi + 1) * n_rows].astype(BF16), vp, preferred_element_type=F32)
    acc_ref[...] = alpha * acc_ref[...] + pv
    m_ref[...] = m_new

    @pl.when(i == pl.num_programs(1) - 1)
    def _():
        o = acc_ref[...] / l_ref[...]
        a = o[:HA] - lam_ref[0] * o[HA:]
        o_ref[...] = (_rms(a, g_ref[...]) * out_scale).astype(o_ref.dtype)


def attn_decode(qm, kown, vown, bias_tile, slope_col, lam, subln_g, cache_k4, cache_v4, page_table_flat, layer,
                pages_per_step):
    bs = qm.shape[0]
    n_pages = page_table_flat.shape[0] // bs
    n_rows = PAGE_SIZE * HA
    lam_init = 0.8 - 0.6 * math.exp(-0.3 * layer)
    kern = functools.partial(_attn_decode_kernel, pages_per_step=pages_per_step,
                             past_len=n_pages * PAGE_SIZE, out_scale=1.0 - lam_init)

    def page_spec(gi):
        return pl.BlockSpec((None, None, n_rows, DV_A),
                            lambda b, i, pt: (layer, pt[b * n_pages + i * pages_per_step + gi], 0, 0))

    grid_spec = pltpu.PrefetchScalarGridSpec(
        num_scalar_prefetch=1,
        grid=(bs, n_pages // pages_per_step),
        in_specs=[
            pl.BlockSpec(memory_space=pltpu.SMEM),
            pl.BlockSpec((None, 2 * HA, DV_A), lambda b, i, pt: (b, 0, 0)),
            pl.BlockSpec((None, 2 * HA, DV_A), lambda b, i, pt: (b, 0, 0)),
            pl.BlockSpec((None, 2 * HA, DV_A), lambda b, i, pt: (b, 0, 0)),
            pl.BlockSpec((2 * HA, n_rows), lambda b, i, pt: (0, 0)),
            pl.BlockSpec((2 * HA, 1), lambda b, i, pt: (0, 0)),
            pl.BlockSpec((None, 1, DV_A), lambda b, i, pt: (layer, 0, 0)),
        ] + [page_spec(gi) for gi in range(pages_per_step)] * 2,
        out_specs=pl.BlockSpec((None, HA, DV_A), lambda b, i, pt: (b, 0, 0)),
        scratch_shapes=[pltpu.VMEM((2 * HA, 1), F32), pltpu.VMEM((2 * HA, 1), F32),
                        pltpu.VMEM((2 * HA, DV_A), F32)],
    )
    return pl.pallas_call(
        kern,
        grid_spec=grid_spec,
        out_shape=jax.ShapeDtypeStruct((bs, HA, DV_A), BF16),
        compiler_params=_cparams(2),
        name="attn_decode",
    )(page_table_flat, lam, qm, kown, vown, bias_tile, slope_col, subln_g,
      *([cache_k4] * pages_per_step), *([cache_v4] * pages_per_step))


def _log_sigmoid(x):
    return jnp.minimum(x, 0.0) - jnp.log1p(jnp.exp(-jnp.abs(x)))


def _mlstm_prompt_kernel(bi_ref, bf_ref, q_ref, k_ref, v_ref, ob_ref, zg_ref, g_ref,
                         hb_ref, c_out_ref, n_out_ref, m_out_ref, c_ref, n_ref, m_ref, *, chunk):
    ci = pl.program_id(1)

    @pl.when(ci == 0)
    def _():
        c_ref[...] = jnp.zeros_like(c_ref)
        n_ref[...] = jnp.zeros_like(n_ref)
        m_ref[...] = jnp.zeros_like(m_ref)

    lane_g = lax.broadcasted_iota(jnp.int32, (1, GATE_COLS), 1)
    gate_bias = jnp.zeros((1, GATE_COLS), F32)
    for h in range(HB):
        gate_bias = jnp.where(lane_g == h, bi_ref[h], jnp.where(lane_g == HB + h, bf_ref[h], gate_bias))
    pre = zg_ref[...] + gate_bias
    gates = jnp.where(lane_g < HB, pre, _log_sigmoid(pre))
    gates_t = gates.T
    row = lax.broadcasted_iota(jnp.int32, (chunk, chunk), 0)
    col = lax.broadcasted_iota(jnp.int32, (chunk, chunk), 1)
    causal = col <= row
    g_norm = g_ref[...]
    q_scale = jnp.asarray(DK ** -0.5, q_ref.dtype)
    heads = range(HB)
    sls = [slice(h * DK, (h + 1) * DK) for h in heads]
    qs = [(q_ref[:, sl] * q_scale).astype(BF16) for sl in sls]
    kb = [k_ref[:, sl].astype(BF16) for sl in sls]
    vb = [v_ref[:, sl].astype(BF16) for sl in sls]
    c_old = [c_ref[h] for h in heads]
    n_old = [n_ref[h:h + 1, :] for h in heads]
    s_raw = [lax.dot_general(qs[h], kb[h], (((1,), (1,)), ((), ())), preferred_element_type=F32) for h in heads]
    inter = [jnp.dot(qs[h], c_old[h].astype(BF16), preferred_element_type=F32) for h in heads]

    b_col, m_t, w_inter, wmat = [], [], [], []
    for h in heads:
        i_row = gates_t[h:h + 1, :]
        lf_col = gates[:, HB + h:HB + h + 1]
        lf_row = gates_t[HB + h:HB + h + 1, :]
        bc = jnp.sum(jnp.where(causal, lf_row, 0.0), axis=1, keepdims=True)
        br = jnp.sum(jnp.where(row <= col, lf_col, 0.0), axis=0, keepdims=True)
        g_col = bc + m_ref[h:h + 1, 0:1]
        dmat = jnp.where(causal, bc - br + i_row, NEG)
        mt = jnp.maximum(g_col, jnp.max(dmat, axis=1, keepdims=True))
        b_col.append(bc)
        m_t.append(mt)
        w_inter.append(jnp.exp(g_col - mt))
        wmat.append(jnp.exp(dmat - mt))

    s = [s_raw[h] * wmat[h] for h in heads]
    sv = [jnp.dot(s[h].astype(BF16), vb[h], preferred_element_type=F32) for h in heads]

    kw = []
    for h in heads:
        m_last = m_t[h][chunk - 1:chunk, :]
        wc_col = jnp.exp(b_col[h][chunk - 1:chunk, :] - b_col[h] + gates[:, h:h + 1] - m_last)
        kw.append(kb[h].astype(F32) * wc_col.astype(BF16).astype(F32))
    kv_new = [jnp.dot(kw[h].T.astype(BF16), vb[h], preferred_element_type=F32) for h in heads]

    for h in heads:
        num = w_inter[h] * inter[h] + sv[h]
        qn = jnp.sum(qs[h].astype(F32) * n_old[h].astype(BF16).astype(F32), axis=-1, keepdims=True)
        den = w_inter[h] * qn + jnp.sum(s[h], axis=-1, keepdims=True)
        hv = num / jnp.maximum(jnp.abs(den), jnp.exp(-m_t[h]))
        hb = _rms(hv, g_norm) * jax.nn.sigmoid(ob_ref[:, sls[h]].astype(F32))
        hb_ref[:, sls[h]] = hb.astype(hb_ref.dtype)
        w_last = w_inter[h][chunk - 1:chunk, :]
        c_ref[h] = w_last * c_old[h] + kv_new[h]
        n_ref[h:h + 1, :] = w_last * n_old[h] + jnp.sum(kw[h], axis=0, keepdims=True)
        m_ref[h:h + 1, :] = jnp.broadcast_to(m_t[h][chunk - 1:chunk, :], (1, m_ref.shape[1]))

    @pl.when(ci == pl.num_programs(1) - 1)
    def _():
        c_out_ref[...] = c_ref[...]
        n_out_ref[...] = n_ref[...]
        m_out_ref[...] = m_ref[...]


def mlstm_prompt(z3, zg3, b_i, b_f, norm_g, layer, chunk):
    b, seq, _ = z3.shape

    def zspec(col):
        return pl.BlockSpec((None, chunk, WB), lambda bi, ci: (bi, ci, col // WB))

    return pl.pallas_call(
        functools.partial(_mlstm_prompt_kernel, chunk=chunk),
        grid=(b, seq // chunk),
        in_specs=[
            pl.BlockSpec(memory_space=pltpu.SMEM),
            pl.BlockSpec(memory_space=pltpu.SMEM),
            zspec(COL_QB), zspec(COL_KB), zspec(COL_VB), zspec(COL_OB),
            pl.BlockSpec((None, chunk, GATE_COLS), lambda bi, ci: (bi, ci, 0)),
            pl.BlockSpec((None, 1, DV_B), lambda bi, ci: (layer, 0, 0)),
        ],
        out_specs=[
            pl.BlockSpec((None, chunk, WB), lambda bi, ci: (bi, ci, 0)),
            pl.BlockSpec((None, HB, DK, DV_B), lambda bi, ci: (bi, 0, 0, 0)),
            pl.BlockSpec((None, HB, DK), lambda bi, ci: (bi, 0, 0)),
            pl.BlockSpec((None, HB, 128), lambda bi, ci: (bi, 0, 0)),
        ],
        out_shape=[
            jax.ShapeDtypeStruct((b, seq, WB), BF16),
            jax.ShapeDtypeStruct((b, HB, DK, DV_B), F32),
            jax.ShapeDtypeStruct((b, HB, DK), F32),
            jax.ShapeDtypeStruct((b, HB, 128), F32),
        ],
        scratch_shapes=[pltpu.VMEM((HB, DK, DV_B), F32), pltpu.VMEM((HB, DK), F32), pltpu.VMEM((HB, 128), F32)],
        compiler_params=_cparams(2),
        name="mlstm_prompt",
    )(b_i, b_f, z3, z3, z3, z3, zg3, norm_g)


def _mlstm_step_kernel(bi_ref, bf_ref, q_ref, kcol_ref, krow_ref, v_ref, ob_ref, zg_ref, g_ref,
                       c0_ref, n0_ref, m0_ref, hb_ref, c_out_ref, n_out_ref, m_out_ref):
    gates = zg_ref[...]
    g_norm = g_ref[...]
    for h in range(HB):
        sl = slice(h * DK, (h + 1) * DK)
        i_pre = gates[:, h:h + 1] + bi_ref[h]
        lf = _log_sigmoid(gates[:, HB + h:HB + h + 1] + bf_ref[h])
        m0 = m0_ref[h:h + 1, 0:1]
        g_st = lf + m0
        m_t = jnp.maximum(g_st, i_pre)
        w_inter = jnp.exp(g_st - m_t)
        w_in = jnp.exp(i_pre - m_t)
        qs = (q_ref[:, sl] * (DK ** -0.5)).astype(BF16)
        k_row = krow_ref[:, sl].astype(BF16).astype(F32)
        k_col = kcol_ref[h].astype(BF16).astype(F32)
        v_row = v_ref[:, sl].astype(BF16).astype(F32)
        c_old = c0_ref[h]
        n_old = n0_ref[h:h + 1, :]
        qf = qs.astype(F32)
        s = jnp.sum(qf * k_row, axis=-1, keepdims=True) * w_in
        q16 = jnp.broadcast_to(qs, (16, DK))
        qc = jnp.dot(q16, c_old.astype(BF16), preferred_element_type=F32)[0:1, :]
        num = w_inter * qc + s.astype(BF16).astype(F32) * v_row
        qn = jnp.sum(qf * n_old.astype(BF16).astype(F32), axis=-1, keepdims=True)
        den = w_inter * qn + s
        hv = num / jnp.maximum(jnp.abs(den), jnp.exp(-m_t))
        hb = _rms(hv, g_norm) * jax.nn.sigmoid(ob_ref[:, sl])
        hb_ref[:, sl] = hb.astype(hb_ref.dtype)
        wb = w_in.astype(BF16).astype(F32)
        kw_col = (k_col * wb).astype(BF16).astype(F32)
        c_out_ref[h] = w_inter * c_old + kw_col * v_row
        n_out_ref[h:h + 1, :] = w_inter * n_old + wb * k_row
        m_out_ref[h:h + 1, :] = jnp.broadcast_to(m_t, (1, m_out_ref.shape[1]))


def mlstm_step(zs3, kcol, zgs3, b_i, b_f, norm_g, state_c, state_n, m0_lanes, layer):
    bs = zs3.shape[0]

    def zspec(col):
        return pl.BlockSpec((None, 1, WB), lambda bi: (bi, 0, col // WB))

    return pl.pallas_call(
        _mlstm_step_kernel,
        grid=(bs,),
        in_specs=[
            pl.BlockSpec(memory_space=pltpu.SMEM),
            pl.BlockSpec(memory_space=pltpu.SMEM),
            zspec(COL_QB),
            pl.BlockSpec((None, HB, DK, 1), lambda bi: (bi, 0, 0, 0)),
            zspec(COL_KB), zspec(COL_VB), zspec(COL_OB),
            pl.BlockSpec((None, 1, GATE_COLS), lambda bi: (bi, 0, 0)),
            pl.BlockSpec((None, 1, DV_B), lambda bi: (layer, 0, 0)),
            pl.BlockSpec((None, None, HB, DK, DV_B), lambda bi: (layer, bi, 0, 0, 0)),
            pl.BlockSpec((None, None, HB, DK), lambda bi: (layer, bi, 0, 0)),
            pl.BlockSpec((None, None, HB, 128), lambda bi: (layer, bi, 0, 0)),
        ],
        out_specs=[
            pl.BlockSpec((None, 1, WB), lambda bi: (bi, 0, 0)),
            pl.BlockSpec((None, HB, DK, DV_B), lambda bi: (bi, 0, 0, 0)),
            pl.BlockSpec((None, HB, DK), lambda bi: (bi, 0, 0)),
            pl.BlockSpec((None, HB, 128), lambda bi: (bi, 0, 0)),
        ],
        out_shape=[
            jax.ShapeDtypeStruct((bs, 1, WB), BF16),
            jax.ShapeDtypeStruct((bs, HB, DK, DV_B), F32),
            jax.ShapeDtypeStruct((bs, HB, DK), F32),
            jax.ShapeDtypeStruct((bs, HB, 128), F32),
        ],
        compiler_params=_cparams(1),
        name="mlstm_step",
    )(b_i, b_f, zs3, kcol, zs3, zs3, zs3, zgs3, norm_g, state_c, state_n, m0_lanes)


def _merge_out_kernel(a_ref, hb_ref, ga_ref, gb_ref, x_ref, wa_ref, wb_ref, wo_ref, o_ref):
    pa = jnp.dot(a_ref[...], wa_ref[...], preferred_element_type=F32)
    pb = jnp.dot(hb_ref[...], wb_ref[...], preferred_element_type=F32)
    merged = jax.nn.sigmoid(ga_ref[...].astype(F32)) * pa + jax.nn.sigmoid(gb_ref[...].astype(F32)) * pb
    o_ref[...] = x_ref[...] + jnp.dot(merged.astype(BF16), wo_ref[...], preferred_element_type=F32)


def merge_out(a, hb, z, x, w_a, w_b, w_o, layer, tm):
    m, d = x.shape
    once = pl.Buffered(1)
    return pl.pallas_call(
        _merge_out_kernel,
        grid=(m // tm,),
        in_specs=[
            pl.BlockSpec((tm, WA), lambda i: (i, 0)),
            pl.BlockSpec((tm, WB), lambda i: (i, 0)),
            pl.BlockSpec((tm, d), lambda i: (i, 0)),
            pl.BlockSpec((tm, d), lambda i: (i, 1)),
            pl.BlockSpec((tm, d), lambda i: (i, 0)),
            pl.BlockSpec((None, WA, d), lambda i: (layer, 0, 0), pipeline_mode=once),
            pl.BlockSpec((None, WB, d), lambda i: (layer, 0, 0), pipeline_mode=once),
            pl.BlockSpec((None, d, d), lambda i: (layer, 0, 0), pipeline_mode=once),
        ],
        out_specs=pl.BlockSpec((tm, d), lambda i: (i, 0)),
        out_shape=jax.ShapeDtypeStruct((m, d), F32),
        compiler_params=_cparams(1),
        name="merge_out",
    )(a, hb, z, z, x, w_a, w_b, w_o)


def _ffn_kernel(x_ref, g_ref, wg_ref, wu_ref, wd_ref, gf_ref, o_ref, h_ref, *, final_norm):
    f = pl.program_id(1)

    @pl.when(f == 0)
    def _():
        x = x_ref[...]
        h_ref[...] = _rms(x, g_ref[...]).astype(BF16)
        o_ref[...] = x

    h = h_ref[...]
    gate = jnp.dot(h, wg_ref[...], preferred_element_type=F32)
    up = jnp.dot(h, wu_ref[...], preferred_element_type=F32)
    act = (gate * jax.nn.sigmoid(gate) * up).astype(BF16)
    o_ref[...] += jnp.dot(act, wd_ref[...], preferred_element_type=F32)

    if final_norm:
        @pl.when(f == pl.num_programs(1) - 1)
        def _():
            o_ref[...] = _rms(o_ref[...], gf_ref[...])


def ffn(x, g, w_gu, w_d, final_g, layer, tm, tf, final_norm):
    m, d = x.shape
    nf = D_FF // tf
    return pl.pallas_call(
        functools.partial(_ffn_kernel, final_norm=final_norm),
        grid=(m // tm, nf),
        in_specs=[
            pl.BlockSpec((tm, d), lambda i, f: (i, 0)),
            pl.BlockSpec((None, 1, d), lambda i, f: (layer, 0, 0)),
            pl.BlockSpec((None, d, tf), lambda i, f: (layer, 0, f)),
            pl.BlockSpec((None, d, tf), lambda i, f: (layer, 0, nf + f)),
            pl.BlockSpec((None, tf, d), lambda i, f: (layer, f, 0)),
            pl.BlockSpec((1, d), lambda i, f: (0, 0)),
        ],
        out_specs=pl.BlockSpec((tm, d), lambda i, f: (i, 0)),
        out_shape=jax.ShapeDtypeStruct((m, d), F32),
        scratch_shapes=[pltpu.VMEM((tm, d), BF16)],
        compiler_params=_cparams(2),
        name="ffn",
    )(x, g, w_gu, w_gu, w_d, final_g)


def kernel(x_prompt, x_sample, cache_k, cache_v, state_C, state_n, state_m, page_table, norm1_g, w_in, b_igate,
           b_fgate, lam_q1, lam_k1, lam_q2, lam_k2, subln_g, mlstm_norm_g, w_proj_a, w_proj_b, w_out, norm2_g,
           w_gu, w_down, final_g):
    bp, lp, d = x_prompt.shape
    bs = x_sample.shape[0]
    n_pool = cache_k.shape[1]
    n_pages = page_table.shape[1]
    past_len = n_pages * PAGE_SIZE
    n_rows = PAGE_SIZE * HA

    w_tail, w_head, w_gate = cast_w_in(jnp.swapaxes(w_in, 1, 2), tn=WA)
    w_a = w_proj_a.astype(BF16)
    w_b = w_proj_b.astype(BF16)
    w_o = w_out.astype(BF16)
    w_gu_b = w_gu.astype(BF16)
    w_d = w_down.astype(BF16)
    norm1 = norm1_g.reshape(DEPTH, 1, d)
    norm2 = norm2_g.reshape(DEPTH, 1, d)
    final_row = final_g.reshape(1, d)
    subln = subln_g.reshape(DEPTH, 1, DV_A)
    subln_col = subln_g.reshape(DEPTH, DV_A, 1)
    mnorm = mlstm_norm_g.reshape(DEPTH, 1, DV_B)

    slopes = jnp.asarray(2.0 ** (-8.0 * np.arange(1, HA + 1) / HA), dtype=F32)
    lam_all = (jnp.exp(jnp.sum(lam_q1 * lam_k1, axis=-1)) - jnp.exp(jnp.sum(lam_q2 * lam_k2, axis=-1))
               + jnp.asarray([0.8 - 0.6 * math.exp(-0.3 * l) for l in range(DEPTH)], dtype=F32))

    key_head = np.arange(n_rows) % HA
    key_tok = np.arange(n_rows) // HA
    q_head = np.arange(2 * HA) % HA
    slopes_np = 2.0 ** (-8.0 * np.arange(1, HA + 1) / HA)
    assert np.all(np.log2(slopes_np) == np.round(np.log2(slopes_np)))
    bias_np = np.where(key_head[None, :] == q_head[:, None], slopes_np[q_head][:, None] * key_tok[None, :], NEG)
    bias_tile = jnp.asarray(bias_np, dtype=F32)
    slope_col = jnp.asarray(slopes_np[q_head][:, None], dtype=F32)
    half_mask = jnp.asarray((np.arange(DV_A)[None, None, :] // QK_HALF) == np.arange(2)[:, None, None])

    cache_k4 = cache_k.reshape(DEPTH, n_pool, n_rows, DV_A)
    cache_v4 = cache_v.reshape(DEPTH, n_pool, n_rows, DV_A)
    pt_flat = page_table.reshape(-1)
    m0_lanes = jnp.broadcast_to(state_m[..., None], state_m.shape + (128,))

    xp = x_prompt.reshape(bp * lp, d)
    xs = x_sample.reshape(bs, d)
    outs = {k: [] for k in ("cp", "np", "mp", "ks", "vs", "cs", "ns", "ms")}
    kv_layers = []
    for l in range(DEPTH):
        lam = lam_all[l:l + 1]
        z, kv, zg = in_proj(xp, norm1, w_tail, w_head, w_gate, l, tm=1024, tn=1024)
        z3 = z.reshape(bp, lp, Z_COLS)
        kv_layers.append(kv)
        a = attn_prompt(z3, slopes, lam, subln_col, l, tq=512, tk=256)
        hb, c_p, n_p, m_p = mlstm_prompt(z3, zg.reshape(bp, lp, GATE_COLS), b_igate[l], b_fgate[l], mnorm, l,
                                         chunk=256)
        outs["cp"].append(c_p)
        outs["np"].append(n_p)
        outs["mp"].append(m_p[:, :, 0])
        xp = merge_out(a.reshape(bp * lp, WA), hb.reshape(bp * lp, WB), z, xp, w_a, w_b, w_o, l, tm=256)
        xp = ffn(xp, norm2, w_gu_b, w_d, final_row, l, tm=1024, tf=512, final_norm=l == DEPTH - 1)

        zs_b, kvs, zgs = in_proj(xs, norm1, w_tail, w_head, w_gate, l, tm=bs, tn=1024)
        zs = zs_b.astype(F32)
        k_new = kvs[:, :WA].reshape(bs, HA, DV_A)
        v_new = kvs[:, WA:].reshape(bs, HA, DV_A)
        outs["ks"].append(k_new.reshape(bs, 1, HA, DV_A))
        outs["vs"].append(v_new.reshape(bs, 1, HA, DV_A))
        q_new = zs[:, COL_QA:COL_QA + WA].reshape(bs, 1, HA, DV_A) * (QK_HALF ** -0.5)
        qm = jnp.where(half_mask[None], q_new, 0.0).reshape(bs, 2 * HA, DV_A).astype(BF16)
        kown = jnp.tile(k_new, (1, 2, 1))
        vown = jnp.tile(v_new, (1, 2, 1))
        a_s = attn_decode(qm, kown, vown, bias_tile, slope_col, lam, subln, cache_k4, cache_v4, pt_flat, l,
                          pages_per_step=8)
        zs3 = zs.reshape(bs, 1, Z_COLS)
        kcol = zs[:, COL_KB:COL_KB + WB].reshape(bs, HB, DK, 1)
        hb_s, c_s, n_s, m_s = mlstm_step(zs3, kcol, zgs.reshape(bs, 1, GATE_COLS), b_igate[l], b_fgate[l], mnorm,
                                         state_C, state_n, m0_lanes, l)
        outs["cs"].append(c_s)
        outs["ns"].append(n_s)
        outs["ms"].append(m_s[:, :, 0])
        xs = merge_out(a_s.reshape(bs, WA), hb_s.reshape(bs, WB), zs_b, xs, w_a, w_b, w_o, l, tm=bs)
        xs = ffn(xs, norm2, w_gu_b, w_d, final_row, l, tm=bs, tf=512, final_norm=l == DEPTH - 1)

    y_prompt = xp.reshape(bp, lp, d)
    y_sample = xs.reshape(bs, 1, d)
    st = {k: jnp.stack(v) for k, v in outs.items()}
    k_all, v_all = kv_layout(kv_layers, tm=256)
    st["kp"] = k_all.reshape(DEPTH, bp, lp, HA, DV_A)
    st["vp"] = v_all.reshape(DEPTH, bp, lp, HA, DV_A)
    return (y_prompt, y_sample, st["kp"], st["vp"], st["cp"], st["np"], st["mp"],
            st["ks"], st["vs"], st["cs"], st["ns"], st["ms"])
```

```python
import functools
import math

import numpy as np
import jax
import jax.numpy as jnp
from jax import lax
from jax.experimental import pallas as pl
from jax.experimental.pallas import tpu as pltpu

F32 = jnp.float32
BF16 = jnp.bfloat16

D_MODEL = 2048
DEPTH = 4
PAGE_SIZE = 128
WA = D_MODEL // 2
DV_A = 128
HA = WA // DV_A
QK_HALF = DV_A // 2
WB = D_MODEL // 2
HB = 4
DK = WB // HB
DV_B = WB // HB
D_FF = 5632
EPS = 1e-6
NEG = -1e30

Z_COLS = 2 * D_MODEL + 3 * WA + 4 * WB
COL_QA = 2 * D_MODEL
COL_KA = COL_QA + WA
COL_VA = COL_KA + WA
COL_QB = COL_VA + WA
COL_KB = COL_QB + WB
COL_VB = COL_KB + WB
COL_OB = COL_VB + WB
GATE_COLS = 128

VMEM_LIMIT = 56 * 1024 * 1024


def _cparams(n_axes, vmem=VMEM_LIMIT):
    return pltpu.CompilerParams(dimension_semantics=("arbitrary",) * n_axes, vmem_limit_bytes=vmem)


def _rms(x, g):
    return x * lax.rsqrt(jnp.mean(x * x, axis=-1, keepdims=True) + EPS) * g


def _cast_w_in_kernel(a_ref, b_ref, wt_ref, wh_ref, wg_ref, *, n_tail, shift):
    j = pl.program_id(1)

    @pl.when(j < n_tail)
    def _():
        wt_ref[...] = jnp.concatenate([a_ref[shift:, :], b_ref[...]], axis=0).astype(BF16)

    @pl.when(j == 0)
    def _():
        pad = jnp.zeros((wg_ref.shape[0] - shift, a_ref.shape[1]), F32)
        wg_ref[...] = jnp.concatenate([a_ref[:shift, :], pad], axis=0).astype(BF16)

    @pl.when(j >= n_tail)
    def _():
        wh_ref[...] = a_ref[...].astype(BF16)


def cast_w_in(w_in_t, tn):
    n_layers, n_in, d = w_in_t.shape
    gate_lo = 3 * WA + 4 * WB
    shift = 2 * HB
    n_head, n_tail = gate_lo // tn, (n_in - gate_lo - shift) // tn
    assert gate_lo % tn == 0 and n_in == gate_lo + shift + n_tail * tn and tn % shift == 0 and shift == 8
    return pl.pallas_call(
        functools.partial(_cast_w_in_kernel, n_tail=n_tail, shift=shift),
        grid=(n_layers, n_tail + n_head),
        in_specs=[
            pl.BlockSpec((None, tn, d), lambda l, j: (l, jnp.where(j < n_tail, n_head + j, j - n_tail), 0)),
            pl.BlockSpec((None, shift, d),
                         lambda l, j: (l, jnp.where(j < n_tail, (n_head + j + 1) * (tn // shift), 0), 0)),
        ],
        out_specs=[
            pl.BlockSpec((None, tn, d), lambda l, j: (l, jnp.minimum(j, n_tail - 1), 0)),
            pl.BlockSpec((None, tn, d), lambda l, j: (l, jnp.maximum(j - n_tail, 0), 0)),
            pl.BlockSpec((None, GATE_COLS, d), lambda l, j: (l, 0, 0)),
        ],
        out_shape=[jax.ShapeDtypeStruct((n_layers, n_tail * tn, d), BF16),
                   jax.ShapeDtypeStruct((n_layers, gate_lo, d), BF16),
                   jax.ShapeDtypeStruct((n_layers, GATE_COLS, d), BF16)],
        compiler_params=_cparams(2),
        name="cast_w_in",
    )(w_in_t, w_in_t)


def _dot_nt(a, b):
    return lax.dot_general(a, b, (((1,), (1,)), ((), ())), preferred_element_type=F32)


def _inproj_kernel(x_ref, g_ref, wt_ref, wh_ref, wg_ref, z_ref, kv_ref, zg_ref, h_ref, *, n_tail, j_k):
    j = pl.program_id(1)

    @pl.when(j == 0)
    def _():
        h = _rms(x_ref[...], g_ref[...]).astype(BF16)
        h_ref[...] = h
        zg_ref[...] = _dot_nt(h, wg_ref[...])

    @pl.when(j < n_tail)
    def _():
        z_ref[...] = _dot_nt(h_ref[...], wt_ref[...]).astype(z_ref.dtype)

    @pl.when(j >= n_tail)
    def _():
        r = _dot_nt(h_ref[...], wh_ref[...])
        z_ref[...] = r.astype(z_ref.dtype)

        @pl.when((j == j_k) | (j == j_k + 1))
        def _():
            kv_ref[...] = r


def in_proj(x, g, w_tail, w_head, w_gate, layer, tm, tn):
    m, d = x.shape
    n_tail = w_tail.shape[1] // tn
    assert tn == WA and COL_VA == COL_KA + WA
    j_k = COL_KA // tn
    return pl.pallas_call(
        functools.partial(_inproj_kernel, n_tail=n_tail, j_k=j_k),
        grid=(m // tm, Z_COLS // tn),
        in_specs=[
            pl.BlockSpec((tm, d), lambda i, j: (i, 0)),
            pl.BlockSpec((None, 1, d), lambda i, j: (layer, 0, 0)),
            pl.BlockSpec((None, tn, d), lambda i, j: (layer, jnp.minimum(j, n_tail - 1), 0)),
            pl.BlockSpec((None, tn, d), lambda i, j: (layer, jnp.maximum(j - n_tail, 0), 0)),
            pl.BlockSpec((None, GATE_COLS, d), lambda i, j: (layer, 0, 0)),
        ],
        out_specs=[
            pl.BlockSpec((tm, tn), lambda i, j: (i, j)),
            pl.BlockSpec((tm, WA), lambda i, j: (i, jnp.clip(j - j_k, 0, 1))),
            pl.BlockSpec((tm, GATE_COLS), lambda i, j: (i, 0)),
        ],
        out_shape=[jax.ShapeDtypeStruct((m, Z_COLS), BF16), jax.ShapeDtypeStruct((m, 2 * WA), F32),
                   jax.ShapeDtypeStruct((m, GATE_COLS), F32)],
        scratch_shapes=[pltpu.VMEM((tm, d), BF16)],
        compiler_params=_cparams(2),
        name="in_proj",
    )(x, g, w_tail, w_head, w_gate)


def _kv_layout_kernel(*refs, tm):
    kv_refs, (ko_ref, vo_ref) = refs[:-2], refs[-2:]
    layer = pl.program_id(0)
    for a, kv_ref in enumerate(kv_refs):
        @pl.when(layer == a)
        def _(kv_ref=kv_ref):
            for h in range(HA):
                ko_ref[pl.ds(h, tm, stride=HA), :] = kv_ref[:, h * DV_A:(h + 1) * DV_A]
                vo_ref[pl.ds(h, tm, stride=HA), :] = kv_ref[:, WA + h * DV_A:WA + (h + 1) * DV_A]


def kv_layout(kvs, tm):
    n_layers = len(kvs)
    m = kvs[0].shape[0]
    n_i = m // tm

    def kv_spec(a):
        return pl.BlockSpec((tm, 2 * WA), lambda l, i: (jnp.where(l == a, i, jnp.where(l < a, 0, n_i - 1)), 0))

    out = jax.ShapeDtypeStruct((n_layers, m * HA, DV_A), F32)
    return pl.pallas_call(
        functools.partial(_kv_layout_kernel, tm=tm),
        grid=(n_layers, n_i),
        in_specs=[kv_spec(a) for a in range(n_layers)],
        out_specs=[pl.BlockSpec((None, tm * HA, DV_A), lambda l, i: (l, i, 0))] * 2,
        out_shape=[out, out],
        compiler_params=_cparams(2),
        name="kv_layout",
    )(*kvs)


def _attn_prompt_kernel(slope_ref, lam_ref, q_ref, k_ref, v_ref, qfeat_ref, kfeat_ref, g_ref, o_ref,
                        kb_ref, vt_ref, s_ref, m_ref, acc_ref, *, tq, tk, out_scale):
    h = pl.program_id(1)
    n_blocks = vt_ref.shape[0]
    n_sub = tq // tk

    kb_ref[:, :DV_A] = k_ref[...].astype(BF16)
    kb_ref[:, DV_A:] = kfeat_ref[...]
    for jb in range(n_blocks):
        vt_ref[jb, :DV_A, :] = v_ref[jb * tk:(jb + 1) * tk, :].astype(F32).T.astype(BF16)
        vt_ref[jb, DV_A:, :] = jnp.ones((vt_ref.shape[1] - DV_A, tk), BF16)

    slope = slope_ref[h]
    lam = lam_ref[0]

    def q_tile(qi, carry):
        q_rows = pl.ds(pl.multiple_of(qi * tq, tq), tq)
        q = q_ref[q_rows, :] * jnp.asarray(QK_HALF ** -0.5, q_ref.dtype)
        lane = lax.broadcasted_iota(jnp.int32, (tq, DV_A), 1)
        qs = jnp.concatenate([jnp.where(lane < QK_HALF, q, 0), jnp.where(lane >= QK_HALF, q, 0)], axis=0)
        qaug = jnp.concatenate([qs.astype(BF16), qfeat_ref[...]], axis=1)

        def segments(u):
            return [(c * tq + u * tk, tq - u * tk) for c in range(2)]

        def scores(jb, segs=((0, 2 * tq),)):
            kj = kb_ref[pl.ds(pl.multiple_of(jb * tk, tk), tk), :]
            qa = jnp.concatenate([qaug[a:a + w] for a, w in segs], axis=0)
            return lax.dot_general(kj, qa, (((1,), (1,)), ((), ())), preferred_element_type=F32)

        def softmax_pv(jb, s, off, segs=((0, 2 * tq),)):
            m = jnp.concatenate([m_ref[:, a:a + w] for a, w in segs], axis=1)
            m_new = jnp.maximum(m, jnp.max(s, axis=0, keepdims=True) - off)
            alpha = jnp.exp(m - m_new)
            p = jnp.exp(s - (m_new + off))
            pv = jnp.dot(vt_ref[jb], p.astype(BF16), preferred_element_type=F32)
            pos = 0
            for a, w in segs:
                acc_ref[:, a:a + w] = alpha[:, pos:pos + w] * acc_ref[:, a:a + w] + pv[:, pos:pos + w]
                m_ref[:, a:a + w] = m_new[:, pos:pos + w]
                pos += w

        def causal(s, width):
            krow = lax.broadcasted_iota(jnp.int32, s.shape, 0)
            qcol = jnp.bitwise_and(lax.broadcasted_iota(jnp.int32, s.shape, 1), width - 1)
            return jnp.where(krow <= qcol, s, NEG)

        def full_blocks(jj, carry):
            for u in range(n_sub):
                jb = jj * n_sub + u
                s_ref[(u + 1) % 2] = scores(jb + 1)
                softmax_pv(jb, s_ref[u % 2], slope * (qi * tq - jb * tk).astype(F32))
            return carry

        s_ref[0] = scores(0)
        m_ref[...] = jnp.full(m_ref.shape, NEG, F32)
        acc_ref[...] = jnp.zeros(acc_ref.shape, F32)
        lax.fori_loop(0, qi, full_blocks, 0)
        for u in range(n_sub):
            jb = qi * n_sub + u
            width = tq - u * tk
            if u + 1 < n_sub:
                s_ref[(u + 1) % 2, :, :2 * (width - tk)] = scores(jb + 1, segments(u + 1))
            softmax_pv(jb, causal(s_ref[u % 2, :, :2 * width], width), slope * (-u * tk), segments(u))
        o = acc_ref[:DV_A, :] / acc_ref[DV_A:DV_A + 1, :]
        a = o[:, :tq] - lam * o[:, tq:]
        y = a * lax.rsqrt(jnp.mean(a * a, axis=0, keepdims=True) + EPS) * g_ref[...] * out_scale
        o_ref[q_rows, :] = y.T.astype(o_ref.dtype)
        return carry

    lax.fori_loop(0, q_ref.shape[0] // tq, q_tile, 0)


def attn_prompt(z3, slopes, lam, subln_col, layer, tq, tk):
    b, seq, _ = z3.shape
    assert tk <= 512 and tq == 2 * tk and tq & (tq - 1) == 0 and tk & (tk - 1) == 0
    lam_init = 0.8 - 0.6 * math.exp(-0.3 * layer)
    kern = functools.partial(_attn_prompt_kernel, tq=tq, tk=tk, out_scale=1.0 - lam_init)
    slopes_np = 2.0 ** (-8.0 * np.arange(1, HA + 1) / HA)
    assert np.all(np.log2(slopes_np) == np.round(np.log2(slopes_np)))
    qpos = np.arange(2 * tq) % tq
    qfeat = np.zeros((HA, 2 * tq, DV_A), np.float32)
    qfeat[:, :, 0] = slopes_np[:, None]
    qfeat[:, :, 1] = -slopes_np[:, None] * (qpos % 256)[None, :]
    qfeat[:, :, 2] = -slopes_np[:, None] * (qpos - qpos % 256)[None, :]
    qfeat[:, :, 3] = slopes_np[:, None]
    kpos = np.arange(seq) % tk
    kfeat = np.zeros((seq, DV_A), np.float32)
    kfeat[:, 0] = kpos % 256
    kfeat[:, 1:3] = 1.0
    kfeat[:, 3] = kpos - kpos % 256
    return pl.pallas_call(
        kern,
        grid=(b, HA),
        in_specs=[
            pl.BlockSpec(memory_space=pltpu.SMEM),
            pl.BlockSpec(memory_space=pltpu.SMEM),
            pl.BlockSpec((None, seq, DV_A), lambda bi, h: (bi, 0, COL_QA // DV_A + h)),
            pl.BlockSpec((None, seq, DV_A), lambda bi, h: (bi, 0, COL_KA // DV_A + h)),
            pl.BlockSpec((None, seq, DV_A), lambda bi, h: (bi, 0, COL_VA // DV_A + h)),
            pl.BlockSpec((None, 2 * tq, DV_A), lambda bi, h: (h, 0, 0)),
            pl.BlockSpec((seq, DV_A), lambda bi, h: (0, 0)),
            pl.BlockSpec((None, DV_A, 1), lambda bi, h: (layer, 0, 0)),
        ],
        out_specs=pl.BlockSpec((None, seq, DV_A), lambda bi, h: (bi, 0, h)),
        out_shape=jax.ShapeDtypeStruct((b, seq, WA), BF16),
        scratch_shapes=[pltpu.VMEM((seq, 2 * DV_A), BF16), pltpu.VMEM((seq // tk, DV_A + 16, tk), BF16),
                        pltpu.VMEM((2, tk, 2 * tq), F32), pltpu.VMEM((1, 2 * tq), F32),
                        pltpu.VMEM((DV_A + 16, 2 * tq), F32)],
        compiler_params=_cparams(2),
        name="attn_prompt",
    )(slopes, lam, z3, z3, z3, jnp.asarray(qfeat, BF16), jnp.asarray(kfeat, BF16), subln_col)


def _attn_decode_kernel(pt_ref, lam_ref, qm_ref, kown_ref, vown_ref, bias_ref, slope_ref, g_ref, *rest,
                        pages_per_step, past_len, out_scale):
    del pt_ref
    k_refs = rest[:pages_per_step]
    v_refs = rest[pages_per_step:2 * pages_per_step]
    o_ref, m_ref, l_ref, acc_ref = rest[2 * pages_per_step:]
    i = pl.program_id(1)
    qm = qm_ref[...]

    @pl.when(i == 0)
    def _():
        ko = kown_ref[...].astype(BF16).astype(F32)
        m_ref[...] = jnp.sum(qm.astype(F32) * ko, axis=-1, keepdims=True)
        l_ref[...] = jnp.ones_like(l_ref)
        acc_ref[...] = vown_ref[...].astype(BF16).astype(F32)

    ss = []
    for gi in range(pages_per_step):
        kp = k_refs[gi][...].astype(BF16)
        s = lax.dot_general(qm, kp, (((1,), (1,)), ((), ())), preferred_element_type=F32)
        page_pos = i * pages_per_step + gi
        off = (page_pos * PAGE_SIZE - past_len).astype(F32)
        ss.append(s + (bias_ref[...] + slope_ref[...] * off))
    s = jnp.concatenate(ss, axis=-1)
    m_old = m_ref[...]
    m_new = jnp.maximum(m_old, jnp.max(s, axis=-1, keepdims=True))
    alpha = jnp.exp(m_old - m_new)
    p = jnp.exp(s - m_new)
    l_ref[...] = alpha * l_ref[...] + jnp.sum(p, axis=-1, keepdims=True)
    pv = jnp.zeros(acc_ref.shape, F32)
    n_rows = PAGE_SIZE * HA
    for gi in range(pages_per_step):
        vp = v_refs[gi][...].astype(BF16)
        pv = pv + jnp.dot(p[:, gi * n_rows:(gi + 1) * n_rows].astype(BF16), vp, preferred_element_type=F32)
    acc_ref[...] = alpha * acc_ref[...] + pv
    m_ref[...] = m_new

    @pl.when(i == pl.num_programs(1) - 1)
    def _():
        o = acc_ref[...] / l_ref[...]
        a = o[:HA] - lam_ref[0] * o[HA:]
        o_ref[...] = (_rms(a, g_ref[...]) * out_scale).astype(o_ref.dtype)


def attn_decode(qm, kown, vown, bias_tile, slope_col, lam, subln_g, cache_k4, cache_v4, page_table_flat, layer,
                pages_per_step):
    bs = qm.shape[0]
    n_pages = page_table_flat.shape[0] // bs
    n_rows = PAGE_SIZE * HA
    lam_init = 0.8 - 0.6 * math.exp(-0.3 * layer)
    kern = functools.partial(_attn_decode_kernel, pages_per_step=pages_per_step,
                             past_len=n_pages * PAGE_SIZE, out_scale=1.0 - lam_init)

    def page_spec(gi):
        return pl.BlockSpec((None, None, n_rows, DV_A),
                            lambda b, i, pt: (layer, pt[b * n_pages + i * pages_per_step + gi], 0, 0))

    grid_spec = pltpu.PrefetchScalarGridSpec(
        num_scalar_prefetch=1,
        grid=(bs, n_pages // pages_per_step),
        in_specs=[
            pl.BlockSpec(memory_space=pltpu.SMEM),
            pl.BlockSpec((None, 2 * HA, DV_A), lambda b, i, pt: (b, 0, 0)),
            pl.BlockSpec((None, 2 * HA, DV_A), lambda b, i, pt: (b, 0, 0)),
            pl.BlockSpec((None, 2 * HA, DV_A), lambda b, i, pt: (b, 0, 0)),
            pl.BlockSpec((2 * HA, n_rows), lambda b, i, pt: (0, 0)),
            pl.BlockSpec((2 * HA, 1), lambda b, i, pt: (0, 0)),
            pl.BlockSpec((None, 1, DV_A), lambda b, i, pt: (layer, 0, 0)),
        ] + [page_spec(gi) for gi in range(pages_per_step)] * 2,
        out_specs=pl.BlockSpec((None, HA, DV_A), lambda b, i, pt: (b, 0, 0)),
        scratch_shapes=[pltpu.VMEM((2 * HA, 1), F32), pltpu.VMEM((2 * HA, 1), F32),
                        pltpu.VMEM((2 * HA, DV_A), F32)],
    )
    return pl.pallas_call(
        kern,
        grid_spec=grid_spec,
        out_shape=jax.ShapeDtypeStruct((bs, HA, DV_A), BF16),
        compiler_params=_cparams(2),
        name="attn_decode",
    )(page_table_flat, lam, qm, kown, vown, bias_tile, slope_col, subln_g,
      *([cache_k4] * pages_per_step), *([cache_v4] * pages_per_step))


def _log_sigmoid(x):
    return jnp.minimum(x, 0.0) - jnp.log1p(jnp.exp(-jnp.abs(x)))


def _mlstm_prompt_kernel(bi_ref, bf_ref, q_ref, k_ref, v_ref, ob_ref, zg_ref, g_ref,
                         hb_ref, c_out_ref, n_out_ref, m_out_ref, c_ref, n_ref, m_ref, *, chunk):
    ci = pl.program_id(1)

    @pl.when(ci == 0)
    def _():
        c_ref[...] = jnp.zeros_like(c_ref)
        n_ref[...] = jnp.zeros_like(n_ref)
        m_ref[...] = jnp.zeros_like(m_ref)

    lane_g = lax.broadcasted_iota(jnp.int32, (1, GATE_COLS), 1)
    gate_bias = jnp.zeros((1, GATE_COLS), F32)
    for h in range(HB):
        gate_bias = jnp.where(lane_g == h, bi_ref[h], jnp.where(lane_g == HB + h, bf_ref[h], gate_bias))
    pre = zg_ref[...] + gate_bias
    gates = jnp.where(lane_g < HB, pre, _log_sigmoid(pre))
    gates_t = gates.T
    row = lax.broadcasted_iota(jnp.int32, (chunk, chunk), 0)
    col = lax.broadcasted_iota(jnp.int32, (chunk, chunk), 1)
    causal = col <= row
    g_norm = g_ref[...]
    q_scale = jnp.asarray(DK ** -0.5, q_ref.dtype)
    heads = range(HB)
    sls = [slice(h * DK, (h + 1) * DK) for h in heads]
    qs = [(q_ref[:, sl] * q_scale).astype(BF16) for sl in sls]
    kb = [k_ref[:, sl].astype(BF16) for sl in sls]
    vb = [v_ref[:, sl].astype(BF16) for sl in sls]
    c_old = [c_ref[h] for h in heads]
    n_old = [n_ref[h:h + 1, :] for h in heads]
    s_raw = [lax.dot_general(qs[h], kb[h], (((1,), (1,)), ((), ())), preferred_element_type=F32) for h in heads]
    inter = [jnp.dot(qs[h], c_old[h].astype(BF16), preferred_element_type=F32) for h in heads]

    b_col, m_t, w_inter, wmat = [], [], [], []
    for h in heads:
        i_row = gates_t[h:h + 1, :]
        lf_col = gates[:, HB + h:HB + h + 1]
        lf_row = gates_t[HB + h:HB + h + 1, :]
        bc = jnp.sum(jnp.where(causal, lf_row, 0.0), axis=1, keepdims=True)
        br = jnp.sum(jnp.where(row <= col, lf_col, 0.0), axis=0, keepdims=True)
        g_col = bc + m_ref[h:h + 1, 0:1]
        dmat = jnp.where(causal, bc - br + i_row, NEG)
        mt = jnp.maximum(g_col, jnp.max(dmat, axis=1, keepdims=True))
        b_col.append(bc)
        m_t.append(mt)
        w_inter.append(jnp.exp(g_col - mt))
        wmat.append(jnp.exp(dmat - mt))

    s = [s_raw[h] * wmat[h] for h in heads]
    sv = [jnp.dot(s[h].astype(BF16), vb[h], preferred_element_type=F32) for h in heads]

    kw = []
    for h in heads:
        m_last = m_t[h][chunk - 1:chunk, :]
        wc_col = jnp.exp(b_col[h][chunk - 1:chunk, :] - b_col[h] + gates[:, h:h + 1] - m_last)
        kw.append(kb[h].astype(F32) * wc_col.astype(BF16).astype(F32))
    kv_new = [jnp.dot(kw[h].T.astype(BF16), vb[h], preferred_element_type=F32) for h in heads]

    for h in heads:
        num = w_inter[h] * inter[h] + sv[h]
        qn = jnp.sum(qs[h].astype(F32) * n_old[h].astype(BF16).astype(F32), axis=-1, keepdims=True)
        den = w_inter[h] * qn + jnp.sum(s[h], axis=-1, keepdims=True)
        hv = num / jnp.maximum(jnp.abs(den), jnp.exp(-m_t[h]))
        hb = _rms(hv, g_norm) * jax.nn.sigmoid(ob_ref[:, sls[h]].astype(F32))
        hb_ref[:, sls[h]] = hb.astype(hb_ref.dtype)
        w_last = w_inter[h][chunk - 1:chunk, :]
        c_ref[h] = w_last * c_old[h] + kv_new[h]
        n_ref[h:h + 1, :] = w_last * n_old[h] + jnp.sum(kw[h], axis=0, keepdims=True)
        m_ref[h:h + 1, :] = jnp.broadcast_to(m_t[h][chunk - 1:chunk, :], (1, m_ref.shape[1]))

    @pl.when(ci == pl.num_programs(1) - 1)
    def _():
        c_out_ref[...] = c_ref[...]
        n_out_ref[...] = n_ref[...]
        m_out_ref[...] = m_ref[...]


def mlstm_prompt(z3, zg3, b_i, b_f, norm_g, layer, chunk):
    b, seq, _ = z3.shape

    def zspec(col):
        return pl.BlockSpec((None, chunk, WB), lambda bi, ci: (bi, ci, col // WB))

    return pl.pallas_call(
        functools.partial(_mlstm_prompt_kernel, chunk=chunk),
        grid=(b, seq // chunk),
        in_specs=[
            pl.BlockSpec(memory_space=pltpu.SMEM),
            pl.BlockSpec(memory_space=pltpu.SMEM),
            zspec(COL_QB), zspec(COL_KB), zspec(COL_VB), zspec(COL_OB),
            pl.BlockSpec((None, chunk, GATE_COLS), lambda bi, ci: (bi, ci, 0)),
            pl.BlockSpec((None, 1, DV_B), lambda bi, ci: (layer, 0, 0)),
        ],
        out_specs=[
            pl.BlockSpec((None, chunk, WB), lambda bi, ci: (bi, ci, 0)),
            pl.BlockSpec((None, HB, DK, DV_B), lambda bi, ci: (bi, 0, 0, 0)),
            pl.BlockSpec((None, HB, DK), lambda bi, ci: (bi, 0, 0)),
            pl.BlockSpec((None, HB, 128), lambda bi, ci: (bi, 0, 0)),
        ],
        out_shape=[
            jax.ShapeDtypeStruct((b, seq, WB), BF16),
            jax.ShapeDtypeStruct((b, HB, DK, DV_B), F32),
            jax.ShapeDtypeStruct((b, HB, DK), F32),
            jax.ShapeDtypeStruct((b, HB, 128), F32),
        ],
        scratch_shapes=[pltpu.VMEM((HB, DK, DV_B), F32), pltpu.VMEM((HB, DK), F32), pltpu.VMEM((HB, 128), F32)],
        compiler_params=_cparams(2),
        name="mlstm_prompt",
    )(b_i, b_f, z3, z3, z3, z3, zg3, norm_g)


def _mlstm_step_kernel(bi_ref, bf_ref, q_ref, kcol_ref, krow_ref, v_ref, ob_ref, zg_ref, g_ref,
                       c0_ref, n0_ref, m0_ref, hb_ref, c_out_ref, n_out_ref, m_out_ref):
    gates = zg_ref[...]
    g_norm = g_ref[...]
    for h in range(HB):
        sl = slice(h * DK, (h + 1) * DK)
        i_pre = gates[:, h:h + 1] + bi_ref[h]
        lf = _log_sigmoid(gates[:, HB + h:HB + h + 1] + bf_ref[h])
        m0 = m0_ref[h:h + 1, 0:1]
        g_st = lf + m0
        m_t = jnp.maximum(g_st, i_pre)
        w_inter = jnp.exp(g_st - m_t)
        w_in = jnp.exp(i_pre - m_t)
        qs = (q_ref[:, sl] * (DK ** -0.5)).astype(BF16)
        k_row = krow_ref[:, sl].astype(BF16).astype(F32)
        k_col = kcol_ref[h].astype(BF16).astype(F32)
        v_row = v_ref[:, sl].astype(BF16).astype(F32)
        c_old = c0_ref[h]
        n_old = n0_ref[h:h + 1, :]
        qf = qs.astype(F32)
        s = jnp.sum(qf * k_row, axis=-1, keepdims=True) * w_in
        q16 = jnp.broadcast_to(qs, (16, DK))
        qc = jnp.dot(q16, c_old.astype(BF16), preferred_element_type=F32)[0:1, :]
        num = w_inter * qc + s.astype(BF16).astype(F32) * v_row
        qn = jnp.sum(qf * n_old.astype(BF16).astype(F32), axis=-1, keepdims=True)
        den = w_inter * qn + s
        hv = num / jnp.maximum(jnp.abs(den), jnp.exp(-m_t))
        hb = _rms(hv, g_norm) * jax.nn.sigmoid(ob_ref[:, sl])
        hb_ref[:, sl] = hb.astype(hb_ref.dtype)
        wb = w_in.astype(BF16).astype(F32)
        kw_col = (k_col * wb).astype(BF16).astype(F32)
        c_out_ref[h] = w_inter * c_old + kw_col * v_row
        n_out_ref[h:h + 1, :] = w_inter * n_old + wb * k_row
        m_out_ref[h:h + 1, :] = jnp.broadcast_to(m_t, (1, m_out_ref.shape[1]))


def mlstm_step(zs3, kcol, zgs3, b_i, b_f, norm_g, state_c, state_n, m0_lanes, layer):
    bs = zs3.shape[0]

    def zspec(col):
        return pl.BlockSpec((None, 1, WB), lambda bi: (bi, 0, col // WB))

    return pl.pallas_call(
        _mlstm_step_kernel,
        grid=(bs,),
        in_specs=[
            pl.BlockSpec(memory_space=pltpu.SMEM),
            pl.BlockSpec(memory_space=pltpu.SMEM),
            zspec(COL_QB),
            pl.BlockSpec((None, HB, DK, 1), lambda bi: (bi, 0, 0, 0)),
            zspec(COL_KB), zspec(COL_VB), zspec(COL_OB),
            pl.BlockSpec((None, 1, GATE_COLS), lambda bi: (bi, 0, 0)),
            pl.BlockSpec((None, 1, DV_B), lambda bi: (layer, 0, 0)),
            pl.BlockSpec((None, None, HB, DK, DV_B), lambda bi: (layer, bi, 0, 0, 0)),
            pl.BlockSpec((None, None, HB, DK), lambda bi: (layer, bi, 0, 0)),
            pl.BlockSpec((None, None, HB, 128), lambda bi: (layer, bi, 0, 0)),
        ],
        out_specs=[
            pl.BlockSpec((None, 1, WB), lambda bi: (bi, 0, 0)),
            pl.BlockSpec((None, HB, DK, DV_B), lambda bi: (bi, 0, 0, 0)),
            pl.BlockSpec((None, HB, DK), lambda bi: (bi, 0, 0)),
            pl.BlockSpec((None, HB, 128), lambda bi: (bi, 0, 0)),
        ],
        out_shape=[
            jax.ShapeDtypeStruct((bs, 1, WB), BF16),
            jax.ShapeDtypeStruct((bs, HB, DK, DV_B), F32),
            jax.ShapeDtypeStruct((bs, HB, DK), F32),
            jax.ShapeDtypeStruct((bs, HB, 128), F32),
        ],
        compiler_params=_cparams(1),
        name="mlstm_step",
    )(b_i, b_f, zs3, kcol, zs3, zs3, zs3, zgs3, norm_g, state_c, state_n, m0_lanes)


def _merge_out_kernel(a_ref, hb_ref, ga_ref, gb_ref, x_ref, wa_ref, wb_ref, wo_ref, o_ref):
    pa = jnp.dot(a_ref[...], wa_ref[...], preferred_element_type=F32)
    pb = jnp.dot(hb_ref[...], wb_ref[...], preferred_element_type=F32)
    merged = jax.nn.sigmoid(ga_ref[...].astype(F32)) * pa + jax.nn.sigmoid(gb_ref[...].astype(F32)) * pb
    o_ref[...] = x_ref[...] + jnp.dot(merged.astype(BF16), wo_ref[...], preferred_element_type=F32)


def merge_out(a, hb, z, x, w_a, w_b, w_o, layer, tm):
    m, d = x.shape
    once = pl.Buffered(1)
    return pl.pallas_call(
        _merge_out_kernel,
        grid=(m // tm,),
        in_specs=[
            pl.BlockSpec((tm, WA), lambda i: (i, 0)),
            pl.BlockSpec((tm, WB), lambda i: (i, 0)),
            pl.BlockSpec((tm, d), lambda i: (i, 0)),
            pl.BlockSpec((tm, d), lambda i: (i, 1)),
            pl.BlockSpec((tm, d), lambda i: (i, 0)),
            pl.BlockSpec((None, WA, d), lambda i: (layer, 0, 0), pipeline_mode=once),
            pl.BlockSpec((None, WB, d), lambda i: (layer, 0, 0), pipeline_mode=once),
            pl.BlockSpec((None, d, d), lambda i: (layer, 0, 0), pipeline_mode=once),
        ],
        out_specs=pl.BlockSpec((tm, d), lambda i: (i, 0)),
        out_shape=jax.ShapeDtypeStruct((m, d), F32),
        compiler_params=_cparams(1),
        name="merge_out",
    )(a, hb, z, z, x, w_a, w_b, w_o)


def _ffn_kernel(x_ref, g_ref, wg_ref, wu_ref, wd_ref, gf_ref, o_ref, h_ref, *, final_norm):
    f = pl.program_id(1)

    @pl.when(f == 0)
    def _():
        x = x_ref[...]
        h_ref[...] = _rms(x, g_ref[...]).astype(BF16)
        o_ref[...] = x

    h = h_ref[...]
    gate = jnp.dot(h, wg_ref[...], preferred_element_type=F32)
    up = jnp.dot(h, wu_ref[...], preferred_element_type=F32)
    act = (gate * jax.nn.sigmoid(gate) * up).astype(BF16)
    o_ref[...] += jnp.dot(act, wd_ref[...], preferred_element_type=F32)

    if final_norm:
        @pl.when(f == pl.num_programs(1) - 1)
        def _():
            o_ref[...] = _rms(o_ref[...], gf_ref[...])


def ffn(x, g, w_gu, w_d, final_g, layer, tm, tf, final_norm):
    m, d = x.shape
    nf = D_FF // tf
    return pl.pallas_call(
        functools.partial(_ffn_kernel, final_norm=final_norm),
        grid=(m // tm, nf),
        in_specs=[
            pl.BlockSpec((tm, d), lambda i, f: (i, 0)),
            pl.BlockSpec((None, 1, d), lambda i, f: (layer, 0, 0)),
            pl.BlockSpec((None, d, tf), lambda i, f: (layer, 0, f)),
            pl.BlockSpec((None, d, tf), lambda i, f: (layer, 0, nf + f)),
            pl.BlockSpec((None, tf, d), lambda i, f: (layer, f, 0)),
            pl.BlockSpec((1, d), lambda i, f: (0, 0)),
        ],
        out_specs=pl.BlockSpec((tm, d), lambda i, f: (i, 0)),
        out_shape=jax.ShapeDtypeStruct((m, d), F32),
        scratch_shapes=[pltpu.VMEM((tm, d), BF16)],
        compiler_params=_cparams(2),
        name="ffn",
    )(x, g, w_gu, w_gu, w_d, final_g)


def kernel(x_prompt, x_sample, cache_k, cache_v, state_C, state_n, state_m, page_table, norm1_g, w_in, b_igate,
           b_fgate, lam_q1, lam_k1, lam_q2, lam_k2, subln_g, mlstm_norm_g, w_proj_a, w_proj_b, w_out, norm2_g,
           w_gu, w_down, final_g):
    bp, lp, d = x_prompt.shape
    bs = x_sample.shape[0]
    n_pool = cache_k.shape[1]
    n_pages = page_table.shape[1]
    past_len = n_pages * PAGE_SIZE
    n_rows = PAGE_SIZE * HA

    w_tail, w_head, w_gate = cast_w_in(jnp.swapaxes(w_in, 1, 2), tn=WA)
    w_a = w_proj_a.astype(BF16)
    w_b = w_proj_b.astype(BF16)
    w_o = w_out.astype(BF16)
    w_gu_b = w_gu.astype(BF16)
    w_d = w_down.astype(BF16)
    norm1 = norm1_g.reshape(DEPTH, 1, d)
    norm2 = norm2_g.reshape(DEPTH, 1, d)
    final_row = final_g.reshape(1, d)
    subln = subln_g.reshape(DEPTH, 1, DV_A)
    subln_col = subln_g.reshape(DEPTH, DV_A, 1)
    mnorm = mlstm_norm_g.reshape(DEPTH, 1, DV_B)

    slopes = jnp.asarray(2.0 ** (-8.0 * np.arange(1, HA + 1) / HA), dtype=F32)
    lam_all = (jnp.exp(jnp.sum(lam_q1 * lam_k1, axis=-1)) - jnp.exp(jnp.sum(lam_q2 * lam_k2, axis=-1))
               + jnp.asarray([0.8 - 0.6 * math.exp(-0.3 * l) for l in range(DEPTH)], dtype=F32))

    key_head = np.arange(n_rows) % HA
    key_tok = np.arange(n_rows) // HA
    q_head = np.arange(2 * HA) % HA
    slopes_np = 2.0 ** (-8.0 * np.arange(1, HA + 1) / HA)
    assert np.all(np.log2(slopes_np) == np.round(np.log2(slopes_np)))
    bias_np = np.where(key_head[None, :] == q_head[:, None], slopes_np[q_head][:, None] * key_tok[None, :], NEG)
    bias_tile = jnp.asarray(bias_np, dtype=F32)
    slope_col = jnp.asarray(slopes_np[q_head][:, None], dtype=F32)
    half_mask = jnp.asarray((np.arange(DV_A)[None, None, :] // QK_HALF) == np.arange(2)[:, None, None])

    cache_k4 = cache_k.reshape(DEPTH, n_pool, n_rows, DV_A)
    cache_v4 = cache_v.reshape(DEPTH, n_pool, n_rows, DV_A)
    pt_flat = page_table.reshape(-1)
    m0_lanes = jnp.broadcast_to(state_m[..., None], state_m.shape + (128,))

    xp = x_prompt.reshape(bp * lp, d)
    xs = x_sample.reshape(bs, d)
    outs = {k: [] for k in ("cp", "np", "mp", "ks", "vs", "cs", "ns", "ms")}
    kv_layers = []
    for l in range(DEPTH):
        lam = lam_all[l:l + 1]
        z, kv, zg = in_proj(xp, norm1, w_tail, w_head, w_gate, l, tm=1024, tn=1024)
        z3 = z.reshape(bp, lp, Z_COLS)
        kv_layers.append(kv)
        a = attn_prompt(z3, slopes, lam, subln_col, l, tq=1024, tk=512)
        hb, c_p, n_p, m_p = mlstm_prompt(z3, zg.reshape(bp, lp, GATE_COLS), b_igate[l], b_fgate[l], mnorm, l,
                                         chunk=256)
        outs["cp"].append(c_p)
        outs["np"].append(n_p)
        outs["mp"].append(m_p[:, :, 0])
        xp = merge_out(a.reshape(bp * lp, WA), hb.reshape(bp * lp, WB), z, xp, w_a, w_b, w_o, l, tm=512)
        xp = ffn(xp, norm2, w_gu_b, w_d, final_row, l, tm=1024, tf=512, final_norm=l == DEPTH - 1)

        zs_b, kvs, zgs = in_proj(xs, norm1, w_tail, w_head, w_gate, l, tm=bs, tn=1024)
        zs = zs_b.astype(F32)
        k_new = kvs[:, :WA].reshape(bs, HA, DV_A)
        v_new = kvs[:, WA:].reshape(bs, HA, DV_A)
        outs["ks"].append(k_new.reshape(bs, 1, HA, DV_A))
        outs["vs"].append(v_new.reshape(bs, 1, HA, DV_A))
        q_new = zs[:, COL_QA:COL_QA + WA].reshape(bs, 1, HA, DV_A) * (QK_HALF ** -0.5)
        qm = jnp.where(half_mask[None], q_new, 0.0).reshape(bs, 2 * HA, DV_A).astype(BF16)
        kown = jnp.tile(k_new, (1, 2, 1))
        vown = jnp.tile(v_new, (1, 2, 1))
        a_s = attn_decode(qm, kown, vown, bias_tile, slope_col, lam, subln, cache_k4, cache_v4, pt_flat, l,
                          pages_per_step=16)
        zs3 = zs.reshape(bs, 1, Z_COLS)
        kcol = zs[:, COL_KB:COL_KB + WB].reshape(bs, HB, DK, 1)
        hb_s, c_s, n_s, m_s = mlstm_step(zs3, kcol, zgs.reshape(bs, 1, GATE_COLS), b_igate[l], b_fgate[l], mnorm,
                                         state_C, state_n, m0_lanes, l)
        outs["cs"].append(c_s)
        outs["ns"].append(n_s)
        outs["ms"].append(m_s[:, :, 0])
        xs = merge_out(a_s.reshape(bs, WA), hb_s.reshape(bs, WB), zs_b, xs, w_a, w_b, w_o, l, tm=bs)
        xs = ffn(xs, norm2, w_gu_b, w_d, final_row, l, tm=bs, tf=512, final_norm=l == DEPTH - 1)

    y_prompt = xp.reshape(bp, lp, d)
    y_sample = xs.reshape(bs, 1, d)
    st = {k: jnp.stack(v) for k, v in outs.items()}
    k_all, v_all = kv_layout(kv_layers, tm=512)
    st["kp"] = k_all.reshape(DEPTH, bp, lp, HA, DV_A)
    st["vp"] = v_all.reshape(DEPTH, bp, lp, HA, DV_A)
    return (y_prompt, y_sample, st["kp"], st["vp"], st["cp"], st["np"], st["mp"],
            st["ks"], st["vs"], st["cs"], st["ns"], st["ms"])
```

```python
import functools
import math

import numpy as np
import jax
import jax.numpy as jnp
from jax import lax
from jax.experimental import pallas as pl
from jax.experimental.pallas import tpu as pltpu

F32 = jnp.float32
BF16 = jnp.bfloat16

D_MODEL = 2048
DEPTH = 4
PAGE_SIZE = 128
WA = D_MODEL // 2
DV_A = 128
HA = WA // DV_A
QK_HALF = DV_A // 2
WB = D_MODEL // 2
HB = 4
DK = WB // HB
DV_B = WB // HB
D_FF = 5632
EPS = 1e-6
NEG = -1e30

Z_COLS = 2 * D_MODEL + 3 * WA + 4 * WB
COL_QA = 2 * D_MODEL
COL_KA = COL_QA + WA
COL_VA = COL_KA + WA
COL_QB = COL_VA + WA
COL_KB = COL_QB + WB
COL_VB = COL_KB + WB
COL_OB = COL_VB + WB
GATE_COLS = 128

VMEM_LIMIT = 56 * 1024 * 1024

TILES = dict(
    in_proj_rows=1024,
    attn_q=1024,
    mlstm_chunk=256,
    merge_rows=512,
    ffn_rows=1024,
    ffn_cols=512,
    decode_pages=16,
    kv_layout_rows=512,
)


def _cparams(n_axes, vmem=VMEM_LIMIT):
    return pltpu.CompilerParams(dimension_semantics=("arbitrary",) * n_axes, vmem_limit_bytes=vmem)


def _rms(x, g):
    return x * lax.rsqrt(jnp.mean(x * x, axis=-1, keepdims=True) + EPS) * g


def _lhs(x, split):
    if not split:
        return x.astype(BF16)
    x = x.astype(F32)
    hi = x.astype(BF16)
    return jnp.concatenate([hi, (x - hi.astype(F32)).astype(BF16)], axis=0)


def _fold(r, split):
    if not split:
        return r
    half = r.shape[0] // 2
    return r[:half] + r[half:]


def _cast_w_in_kernel(a_ref, b_ref, wt_ref, wh_ref, wg_ref, *, n_tail, shift):
    j = pl.program_id(1)

    @pl.when(j < n_tail)
    def _():
        wt_ref[...] = jnp.concatenate([a_ref[shift:, :], b_ref[...]], axis=0).astype(BF16)

    @pl.when(j == 0)
    def _():
        pad = jnp.zeros((wg_ref.shape[0] - shift, a_ref.shape[1]), F32)
        wg_ref[...] = jnp.concatenate([a_ref[:shift, :], pad], axis=0).astype(BF16)

    @pl.when(j >= n_tail)
    def _():
        wh_ref[...] = a_ref[...].astype(BF16)


def cast_w_in(w_in_t, tn):
    n_layers, n_in, d = w_in_t.shape
    gate_lo = 3 * WA + 4 * WB
    shift = 2 * HB
    n_head, n_tail = gate_lo // tn, (n_in - gate_lo - shift) // tn
    assert gate_lo % tn == 0 and n_in == gate_lo + shift + n_tail * tn and tn % shift == 0 and shift == 8
    return pl.pallas_call(
        functools.partial(_cast_w_in_kernel, n_tail=n_tail, shift=shift),
        grid=(n_layers, n_tail + n_head),
        in_specs=[
            pl.BlockSpec((None, tn, d), lambda l, j: (l, jnp.where(j < n_tail, n_head + j, j - n_tail), 0)),
            pl.BlockSpec((None, shift, d),
                         lambda l, j: (l, jnp.where(j < n_tail, (n_head + j + 1) * (tn // shift), 0), 0)),
        ],
        out_specs=[
            pl.BlockSpec((None, tn, d), lambda l, j: (l, jnp.minimum(j, n_tail - 1), 0)),
            pl.BlockSpec((None, tn, d), lambda l, j: (l, jnp.maximum(j - n_tail, 0), 0)),
            pl.BlockSpec((None, GATE_COLS, d), lambda l, j: (l, 0, 0)),
        ],
        out_shape=[jax.ShapeDtypeStruct((n_layers, n_tail * tn, d), BF16),
                   jax.ShapeDtypeStruct((n_layers, gate_lo, d), BF16),
                   jax.ShapeDtypeStruct((n_layers, GATE_COLS, d), BF16)],
        compiler_params=_cparams(2),
        name="cast_w_in",
    )(w_in_t, w_in_t)


def _dot_nt(a, b):
    return lax.dot_general(a, b, (((1,), (1,)), ((), ())), preferred_element_type=F32)


def _inproj_kernel(x_ref, g_ref, wt_ref, wh_ref, wg_ref, z_ref, kv_ref, zg_ref, h_ref, *, n_tail, j_k, split):
    j = pl.program_id(1)

    @pl.when(j == 0)
    def _():
        h = _lhs(_rms(x_ref[...], g_ref[...]), split)
        h_ref[...] = h
        zg_ref[...] = _fold(_dot_nt(h, wg_ref[...]), split)

    @pl.when(j < n_tail)
    def _():
        z_ref[...] = _fold(_dot_nt(h_ref[...], wt_ref[...]), split).astype(z_ref.dtype)

    @pl.when(j >= n_tail)
    def _():
        r = _fold(_dot_nt(h_ref[...], wh_ref[...]), split)
        z_ref[...] = r.astype(z_ref.dtype)

        @pl.when((j == j_k) | (j == j_k + 1))
        def _():
            kv_ref[...] = r


def in_proj(x, g, w_tail, w_head, w_gate, layer, tm, tn, split=False):
    m, d = x.shape
    n_tail = w_tail.shape[1] // tn
    assert tn == WA and COL_VA == COL_KA + WA
    j_k = COL_KA // tn
    return pl.pallas_call(
        functools.partial(_inproj_kernel, n_tail=n_tail, j_k=j_k, split=split),
        grid=(m // tm, Z_COLS // tn),
        in_specs=[
            pl.BlockSpec((tm, d), lambda i, j: (i, 0)),
            pl.BlockSpec((None, 1, d), lambda i, j: (layer, 0, 0)),
            pl.BlockSpec((None, tn, d), lambda i, j: (layer, jnp.minimum(j, n_tail - 1), 0)),
            pl.BlockSpec((None, tn, d), lambda i, j: (layer, jnp.maximum(j - n_tail, 0), 0)),
            pl.BlockSpec((None, GATE_COLS, d), lambda i, j: (layer, 0, 0)),
        ],
        out_specs=[
            pl.BlockSpec((tm, tn), lambda i, j: (i, j)),
            pl.BlockSpec((tm, WA), lambda i, j: (i, jnp.clip(j - j_k, 0, 1))),
            pl.BlockSpec((tm, GATE_COLS), lambda i, j: (i, 0)),
        ],
        out_shape=[jax.ShapeDtypeStruct((m, Z_COLS), F32 if split else BF16), jax.ShapeDtypeStruct((m, 2 * WA), F32),
                   jax.ShapeDtypeStruct((m, GATE_COLS), F32)],
        scratch_shapes=[pltpu.VMEM((2 * tm if split else tm, d), BF16)],
        compiler_params=_cparams(2),
        name="in_proj",
    )(x, g, w_tail, w_head, w_gate)


def _kv_layout_kernel(*refs, tm):
    kv_refs, (ko_ref, vo_ref) = refs[:-2], refs[-2:]
    layer = pl.program_id(0)
    for a, kv_ref in enumerate(kv_refs):
        @pl.when(layer == a)
        def _(kv_ref=kv_ref):
            for h in range(HA):
                ko_ref[pl.ds(h, tm, stride=HA), :] = kv_ref[:, h * DV_A:(h + 1) * DV_A]
                vo_ref[pl.ds(h, tm, stride=HA), :] = kv_ref[:, WA + h * DV_A:WA + (h + 1) * DV_A]


def kv_layout(kvs, tm):
    n_layers = len(kvs)
    m = kvs[0].shape[0]
    n_i = m // tm

    def kv_spec(a):
        return pl.BlockSpec((tm, 2 * WA), lambda l, i: (jnp.where(l == a, i, jnp.where(l < a, 0, n_i - 1)), 0))

    out = jax.ShapeDtypeStruct((n_layers, m * HA, DV_A), F32)
    return pl.pallas_call(
        functools.partial(_kv_layout_kernel, tm=tm),
        grid=(n_layers, n_i),
        in_specs=[kv_spec(a) for a in range(n_layers)],
        out_specs=[pl.BlockSpec((None, tm * HA, DV_A), lambda l, i: (l, i, 0))] * 2,
        out_shape=[out, out],
        compiler_params=_cparams(2),
        name="kv_layout",
    )(*kvs)


def _attn_prompt_kernel(slope_ref, lam_ref, q_ref, k_ref, v_ref, qfeat_ref, kfeat_ref, g_ref, o_ref,
                        kb_ref, vt_ref, s_ref, m_ref, acc_ref, *, tq, tk, out_scale):
    h = pl.program_id(1)
    n_blocks = vt_ref.shape[0]
    n_sub = tq // tk

    kb_ref[:, :DV_A] = k_ref[...].astype(BF16)
    kb_ref[:, DV_A:] = kfeat_ref[...]
    for jb in range(n_blocks):
        vt_ref[jb, :DV_A, :] = v_ref[jb * tk:(jb + 1) * tk, :].astype(F32).T.astype(BF16)
        vt_ref[jb, DV_A:, :] = jnp.ones((vt_ref.shape[1] - DV_A, tk), BF16)

    slope = slope_ref[h]
    lam = lam_ref[0]

    def q_tile(qi, carry):
        q_rows = pl.ds(pl.multiple_of(qi * tq, tq), tq)
        q = q_ref[q_rows, :] * jnp.asarray(QK_HALF ** -0.5, q_ref.dtype)
        lane = lax.broadcasted_iota(jnp.int32, (tq, DV_A), 1)
        qs = jnp.concatenate([jnp.where(lane < QK_HALF, q, 0), jnp.where(lane >= QK_HALF, q, 0)], axis=0)
        qaug = jnp.concatenate([qs.astype(BF16), qfeat_ref[...]], axis=1)

        def segments(u):
            return [(c * tq + u * tk, tq - u * tk) for c in range(2)]

        def scores(jb, segs=((0, 2 * tq),)):
            kj = kb_ref[pl.ds(pl.multiple_of(jb * tk, tk), tk), :]
            qa = jnp.concatenate([qaug[a:a + w] for a, w in segs], axis=0)
            return lax.dot_general(kj, qa, (((1,), (1,)), ((), ())), preferred_element_type=F32)

        def softmax_pv(jb, s, off, segs=((0, 2 * tq),)):
            m = jnp.concatenate([m_ref[:, a:a + w] for a, w in segs], axis=1)
            m_new = jnp.maximum(m, jnp.max(s, axis=0, keepdims=True) - off)
            alpha = jnp.exp(m - m_new)
            p = jnp.exp(s - (m_new + off))
            pv = jnp.dot(vt_ref[jb], p.astype(BF16), preferred_element_type=F32)
            pos = 0
            for a, w in segs:
                acc_ref[:, a:a + w] = alpha[:, pos:pos + w] * acc_ref[:, a:a + w] + pv[:, pos:pos + w]
                m_ref[:, a:a + w] = m_new[:, pos:pos + w]
                pos += w

        def causal(s, width):
            krow = lax.broadcasted_iota(jnp.int32, s.shape, 0)
            qcol = jnp.bitwise_and(lax.broadcasted_iota(jnp.int32, s.shape, 1), width - 1)
            return jnp.where(krow <= qcol, s, NEG)

        def full_blocks(jj, carry):
            for u in range(n_sub):
                jb = jj * n_sub + u
                s_ref[(u + 1) % 2] = scores(jb + 1)
                softmax_pv(jb, s_ref[u % 2], slope * (qi * tq - jb * tk).astype(F32))
            return carry

        s_ref[0] = scores(0)
        m_ref[...] = jnp.full(m_ref.shape, NEG, F32)
        acc_ref[...] = jnp.zeros(acc_ref.shape, F32)
        lax.fori_loop(0, qi, full_blocks, 0)
        for u in range(n_sub):
            jb = qi * n_sub + u
            width = tq - u * tk
            if u + 1 < n_sub:
                s_ref[(u + 1) % 2, :, :2 * (width - tk)] = scores(jb + 1, segments(u + 1))
            softmax_pv(jb, causal(s_ref[u % 2, :, :2 * width], width), slope * (-u * tk), segments(u))
        o = acc_ref[:DV_A, :] / acc_ref[DV_A:DV_A + 1, :]
        a = o[:, :tq] - lam * o[:, tq:]
        y = a * lax.rsqrt(jnp.mean(a * a, axis=0, keepdims=True) + EPS) * g_ref[...] * out_scale
        o_ref[q_rows, :] = y.T.astype(o_ref.dtype)
        return carry

    lax.fori_loop(0, q_ref.shape[0] // tq, q_tile, 0)


def attn_prompt(z3, slopes, lam, subln_col, layer, tq, tk):
    b, seq, _ = z3.shape
    assert tk <= 512 and tq == 2 * tk and tq & (tq - 1) == 0 and tk & (tk - 1) == 0
    lam_init = 0.8 - 0.6 * math.exp(-0.3 * layer)
    kern = functools.partial(_attn_prompt_kernel, tq=tq, tk=tk, out_scale=1.0 - lam_init)
    slopes_np = 2.0 ** (-8.0 * np.arange(1, HA + 1) / HA)
    assert np.all(np.log2(slopes_np) == np.round(np.log2(slopes_np)))
    qpos = np.arange(2 * tq) % tq
    qfeat = np.zeros((HA, 2 * tq, DV_A), np.float32)
    qfeat[:, :, 0] = slopes_np[:, None]
    qfeat[:, :, 1] = -slopes_np[:, None] * (qpos % 256)[None, :]
    qfeat[:, :, 2] = -slopes_np[:, None] * (qpos - qpos % 256)[None, :]
    qfeat[:, :, 3] = slopes_np[:, None]
    kpos = np.arange(seq) % tk
    kfeat = np.zeros((seq, DV_A), np.float32)
    kfeat[:, 0] = kpos % 256
    kfeat[:, 1:3] = 1.0
    kfeat[:, 3] = kpos - kpos % 256
    return pl.pallas_call(
        kern,
        grid=(b, HA),
        in_specs=[
            pl.BlockSpec(memory_space=pltpu.SMEM),
            pl.BlockSpec(memory_space=pltpu.SMEM),
            pl.BlockSpec((None, seq, DV_A), lambda bi, h: (bi, 0, COL_QA // DV_A + h)),
            pl.BlockSpec((None, seq, DV_A), lambda bi, h: (bi, 0, COL_KA // DV_A + h)),
            pl.BlockSpec((None, seq, DV_A), lambda bi, h: (bi, 0, COL_VA // DV_A + h)),
            pl.BlockSpec((None, 2 * tq, DV_A), lambda bi, h: (h, 0, 0)),
            pl.BlockSpec((seq, DV_A), lambda bi, h: (0, 0)),
            pl.BlockSpec((None, DV_A, 1), lambda bi, h: (layer, 0, 0)),
        ],
        out_specs=pl.BlockSpec((None, seq, DV_A), lambda bi, h: (bi, 0, h)),
        out_shape=jax.ShapeDtypeStruct((b, seq, WA), BF16),
        scratch_shapes=[pltpu.VMEM((seq, 2 * DV_A), BF16), pltpu.VMEM((seq // tk, DV_A + 16, tk), BF16),
                        pltpu.VMEM((2, tk, 2 * tq), F32), pltpu.VMEM((1, 2 * tq), F32),
                        pltpu.VMEM((DV_A + 16, 2 * tq), F32)],
        compiler_params=_cparams(2),
        name="attn_prompt",
    )(slopes, lam, z3, z3, z3, jnp.asarray(qfeat, BF16), jnp.asarray(kfeat, BF16), subln_col)


def _attn_decode_kernel(pt_ref, lam_ref, qm_ref, kown_ref, vown_ref, bias_ref, slope_ref, g_ref, *rest,
                        pages_per_step, past_len, out_scale):
    del pt_ref
    k_refs = rest[:pages_per_step]
    v_refs = rest[pages_per_step:2 * pages_per_step]
    o_ref, m_ref, l_ref, acc_ref = rest[2 * pages_per_step:]
    i = pl.program_id(1)
    qm = qm_ref[...]

    @pl.when(i == 0)
    def _():
        ko = kown_ref[...].astype(BF16).astype(F32)
        m_ref[...] = jnp.sum(qm.astype(F32) * ko, axis=-1, keepdims=True)
        l_ref[...] = jnp.ones_like(l_ref)
        acc_ref[...] = vown_ref[...].astype(BF16).astype(F32)

    ss = []
    for gi in range(pages_per_step):
        kp = k_refs[gi][...].astype(BF16)
        s = lax.dot_general(qm, kp, (((1,), (1,)), ((), ())), preferred_element_type=F32)
        page_pos = i * pages_per_step + gi
        off = (page_pos * PAGE_SIZE - past_len).astype(F32)
        ss.append(s + (bias_ref[...] + slope_ref[...] * off))
    s = jnp.concatenate(ss, axis=-1)
    m_old = m_ref[...]
    m_new = jnp.maximum(m_old, jnp.max(s, axis=-1, keepdims=True))
    alpha = jnp.exp(m_old - m_new)
    p = jnp.exp(s - m_new)
    l_ref[...] = alpha * l_ref[...] + jnp.sum(p, axis=-1, keepdims=True)
    pv = jnp.zeros(acc_ref.shape, F32)
    n_rows = PAGE_SIZE * HA
    for gi in range(pages_per_step):
        vp = v_refs[gi][...].astype(BF16)
        pv = pv + jnp.dot(p[:, gi * n_rows:(gi + 1) * n_rows].astype(BF16), vp, preferred_element_type=F32)
    acc_ref[...] = alpha * acc_ref[...] + pv
    m_ref[...] = m_new

    @pl.when(i == pl.num_programs(1) - 1)
    def _():
        o = acc_ref[...] / l_ref[...]
        a = o[:HA] - lam_ref[0] * o[HA:]
        o_ref[...] = (_rms(a, g_ref[...]) * out_scale).astype(o_ref.dtype)


def attn_decode(qm, kown, vown, bias_tile, slope_col, lam, subln_g, cache_k4, cache_v4, page_table_flat, layer,
                pages_per_step):
    bs = qm.shape[0]
    n_pages = page_table_flat.shape[0] // bs
    n_rows = PAGE_SIZE * HA
    lam_init = 0.8 - 0.6 * math.exp(-0.3 * layer)
    kern = functools.partial(_attn_decode_kernel, pages_per_step=pages_per_step,
                             past_len=n_pages * PAGE_SIZE, out_scale=1.0 - lam_init)

    def page_spec(gi):
        return pl.BlockSpec((None, None, n_rows, DV_A),
                            lambda b, i, pt: (layer, pt[b * n_pages + i * pages_per_step + gi], 0, 0))

    grid_spec = pltpu.PrefetchScalarGridSpec(
        num_scalar_prefetch=1,
        grid=(bs, n_pages // pages_per_step),
        in_specs=[
            pl.BlockSpec(memory_space=pltpu.SMEM),
            pl.BlockSpec((None, 2 * HA, DV_A), lambda b, i, pt: (b, 0, 0)),
            pl.BlockSpec((None, 2 * HA, DV_A), lambda b, i, pt: (b, 0, 0)),
            pl.BlockSpec((None, 2 * HA, DV_A), lambda b, i, pt: (b, 0, 0)),
            pl.BlockSpec((2 * HA, n_rows), lambda b, i, pt: (0, 0)),
            pl.BlockSpec((2 * HA, 1), lambda b, i, pt: (0, 0)),
            pl.BlockSpec((None, 1, DV_A), lambda b, i, pt: (layer, 0, 0)),
        ] + [page_spec(gi) for gi in range(pages_per_step)] * 2,
        out_specs=pl.BlockSpec((None, HA, DV_A), lambda b, i, pt: (b, 0, 0)),
        scratch_shapes=[pltpu.VMEM((2 * HA, 1), F32), pltpu.VMEM((2 * HA, 1), F32),
                        pltpu.VMEM((2 * HA, DV_A), F32)],
    )
    return pl.pallas_call(
        kern,
        grid_spec=grid_spec,
        out_shape=jax.ShapeDtypeStruct((bs, HA, DV_A), F32),
        compiler_params=_cparams(2),
        name="attn_decode",
    )(page_table_flat, lam, qm, kown, vown, bias_tile, slope_col, subln_g,
      *([cache_k4] * pages_per_step), *([cache_v4] * pages_per_step))


def _log_sigmoid(x):
    return jnp.minimum(x, 0.0) - jnp.log1p(jnp.exp(-jnp.abs(x)))


def _mlstm_prompt_kernel(bi_ref, bf_ref, q_ref, k_ref, v_ref, ob_ref, zg_ref, g_ref,
                         hb_ref, c_out_ref, n_out_ref, m_out_ref, c_ref, n_ref, m_ref, *, chunk):
    ci = pl.program_id(1)

    @pl.when(ci == 0)
    def _():
        c_ref[...] = jnp.zeros_like(c_ref)
        n_ref[...] = jnp.zeros_like(n_ref)
        m_ref[...] = jnp.zeros_like(m_ref)

    lane_g = lax.broadcasted_iota(jnp.int32, (1, GATE_COLS), 1)
    gate_bias = jnp.zeros((1, GATE_COLS), F32)
    for h in range(HB):
        gate_bias = jnp.where(lane_g == h, bi_ref[h], jnp.where(lane_g == HB + h, bf_ref[h], gate_bias))
    pre = zg_ref[...] + gate_bias
    gates = jnp.where(lane_g < HB, pre, _log_sigmoid(pre))
    gates_t = gates.T
    row = lax.broadcasted_iota(jnp.int32, (chunk, chunk), 0)
    col = lax.broadcasted_iota(jnp.int32, (chunk, chunk), 1)
    causal = col <= row
    g_norm = g_ref[...]
    q_scale = jnp.asarray(DK ** -0.5, q_ref.dtype)
    heads = range(HB)
    sls = [slice(h * DK, (h + 1) * DK) for h in heads]
    qs = [(q_ref[:, sl] * q_scale).astype(BF16) for sl in sls]
    kb = [k_ref[:, sl].astype(BF16) for sl in sls]
    vb = [v_ref[:, sl].astype(BF16) for sl in sls]
    c_old = [c_ref[h] for h in heads]
    n_old = [n_ref[h:h + 1, :] for h in heads]
    s_raw = [lax.dot_general(qs[h], kb[h], (((1,), (1,)), ((), ())), preferred_element_type=F32) for h in heads]
    inter = [jnp.dot(qs[h], c_old[h].astype(BF16), preferred_element_type=F32) for h in heads]

    b_col, m_t, w_inter, wmat = [], [], [], []
    for h in heads:
        i_row = gates_t[h:h + 1, :]
        lf_col = gates[:, HB + h:HB + h + 1]
        lf_row = gates_t[HB + h:HB + h + 1, :]
        bc = jnp.sum(jnp.where(causal, lf_row, 0.0), axis=1, keepdims=True)
        br = jnp.sum(jnp.where(row <= col, lf_col, 0.0), axis=0, keepdims=True)
        g_col = bc + m_ref[h:h + 1, 0:1]
        dmat = jnp.where(causal, bc - br + i_row, NEG)
        mt = jnp.maximum(g_col, jnp.max(dmat, axis=1, keepdims=True))
        b_col.append(bc)
        m_t.append(mt)
        w_inter.append(jnp.exp(g_col - mt))
        wmat.append(jnp.exp(dmat - mt))

    s = [s_raw[h] * wmat[h] for h in heads]
    sv = [jnp.dot(s[h].astype(BF16), vb[h], preferred_element_type=F32) for h in heads]

    kw = []
    for h in heads:
        m_last = m_t[h][chunk - 1:chunk, :]
        wc_col = jnp.exp(b_col[h][chunk - 1:chunk, :] - b_col[h] + gates[:, h:h + 1] - m_last)
        kw.append(kb[h].astype(F32) * wc_col.astype(BF16).astype(F32))
    kv_new = [jnp.dot(kw[h].T.astype(BF16), vb[h], preferred_element_type=F32) for h in heads]

    for h in heads:
        num = w_inter[h] * inter[h] + sv[h]
        qn = jnp.sum(qs[h].astype(F32) * n_old[h].astype(BF16).astype(F32), axis=-1, keepdims=True)
        den = w_inter[h] * qn + jnp.sum(s[h], axis=-1, keepdims=True)
        hv = num / jnp.maximum(jnp.abs(den), jnp.exp(-m_t[h]))
        hb = _rms(hv, g_norm) * jax.nn.sigmoid(ob_ref[:, sls[h]].astype(F32))
        hb_ref[:, sls[h]] = hb.astype(hb_ref.dtype)
        w_last = w_inter[h][chunk - 1:chunk, :]
        c_ref[h] = w_last * c_old[h] + kv_new[h]
        n_ref[h:h + 1, :] = w_last * n_old[h] + jnp.sum(kw[h], axis=0, keepdims=True)
        m_ref[h:h + 1, :] = jnp.broadcast_to(m_t[h][chunk - 1:chunk, :], (1, m_ref.shape[1]))

    @pl.when(ci == pl.num_programs(1) - 1)
    def _():
        c_out_ref[...] = c_ref[...]
        n_out_ref[...] = n_ref[...]
        m_out_ref[...] = m_ref[...]


def mlstm_prompt(z3, zg3, b_i, b_f, norm_g, layer, chunk):
    b, seq, _ = z3.shape

    def zspec(col):
        return pl.BlockSpec((None, chunk, WB), lambda bi, ci: (bi, ci, col // WB))

    return pl.pallas_call(
        functools.partial(_mlstm_prompt_kernel, chunk=chunk),
        grid=(b, seq // chunk),
        in_specs=[
            pl.BlockSpec(memory_space=pltpu.SMEM),
            pl.BlockSpec(memory_space=pltpu.SMEM),
            zspec(COL_QB), zspec(COL_KB), zspec(COL_VB), zspec(COL_OB),
            pl.BlockSpec((None, chunk, GATE_COLS), lambda bi, ci: (bi, ci, 0)),
            pl.BlockSpec((None, 1, DV_B), lambda bi, ci: (layer, 0, 0)),
        ],
        out_specs=[
            pl.BlockSpec((None, chunk, WB), lambda bi, ci: (bi, ci, 0)),
            pl.BlockSpec((None, HB, DK, DV_B), lambda bi, ci: (bi, 0, 0, 0)),
            pl.BlockSpec((None, HB, DK), lambda bi, ci: (bi, 0, 0)),
            pl.BlockSpec((None, HB, 128), lambda bi, ci: (bi, 0, 0)),
        ],
        out_shape=[
            jax.ShapeDtypeStruct((b, seq, WB), BF16),
            jax.ShapeDtypeStruct((b, HB, DK, DV_B), F32),
            jax.ShapeDtypeStruct((b, HB, DK), F32),
            jax.ShapeDtypeStruct((b, HB, 128), F32),
        ],
        scratch_shapes=[pltpu.VMEM((HB, DK, DV_B), F32), pltpu.VMEM((HB, DK), F32), pltpu.VMEM((HB, 128), F32)],
        compiler_params=_cparams(2),
        name="mlstm_prompt",
    )(b_i, b_f, z3, z3, z3, z3, zg3, norm_g)


def _mlstm_step_kernel(bi_ref, bf_ref, q_ref, kcol_ref, krow_ref, v_ref, ob_ref, zg_ref, g_ref,
                       c0_ref, n0_ref, m0_ref, hb_ref, c_out_ref, n_out_ref, m_out_ref):
    gates = zg_ref[...]
    g_norm = g_ref[...]
    for h in range(HB):
        sl = slice(h * DK, (h + 1) * DK)
        i_pre = gates[:, h:h + 1] + bi_ref[h]
        lf = _log_sigmoid(gates[:, HB + h:HB + h + 1] + bf_ref[h])
        m0 = m0_ref[h:h + 1, 0:1]
        g_st = lf + m0
        m_t = jnp.maximum(g_st, i_pre)
        w_inter = jnp.exp(g_st - m_t)
        w_in = jnp.exp(i_pre - m_t)
        qs = (q_ref[:, sl] * (DK ** -0.5)).astype(BF16)
        k_row = krow_ref[:, sl].astype(BF16).astype(F32)
        k_col = kcol_ref[h].astype(BF16).astype(F32)
        v_row = v_ref[:, sl].astype(BF16).astype(F32)
        c_old = c0_ref[h]
        n_old = n0_ref[h:h + 1, :]
        qf = qs.astype(F32)
        s = jnp.sum(qf * k_row, axis=-1, keepdims=True) * w_in
        q16 = jnp.broadcast_to(qs, (16, DK))
        qc = jnp.dot(q16, c_old.astype(BF16), preferred_element_type=F32)[0:1, :]
        num = w_inter * qc + s.astype(BF16).astype(F32) * v_row
        qn = jnp.sum(qf * n_old.astype(BF16).astype(F32), axis=-1, keepdims=True)
        den = w_inter * qn + s
        hv = num / jnp.maximum(jnp.abs(den), jnp.exp(-m_t))
        hb = _rms(hv, g_norm) * jax.nn.sigmoid(ob_ref[:, sl])
        hb_ref[:, sl] = hb.astype(hb_ref.dtype)
        wb = w_in.astype(BF16).astype(F32)
        kw_col = (k_col * wb).astype(BF16).astype(F32)
        c_out_ref[h] = w_inter * c_old + kw_col * v_row
        n_out_ref[h:h + 1, :] = w_inter * n_old + wb * k_row
        m_out_ref[h:h + 1, :] = jnp.broadcast_to(m_t, (1, m_out_ref.shape[1]))


def mlstm_step(zs3, kcol, zgs3, b_i, b_f, norm_g, state_c, state_n, m0_lanes, layer):
    bs = zs3.shape[0]

    def zspec(col):
        return pl.BlockSpec((None, 1, WB), lambda bi: (bi, 0, col // WB))

    return pl.pallas_call(
        _mlstm_step_kernel,
        grid=(bs,),
        in_specs=[
            pl.BlockSpec(memory_space=pltpu.SMEM),
            pl.BlockSpec(memory_space=pltpu.SMEM),
            zspec(COL_QB),
            pl.BlockSpec((None, HB, DK, 1), lambda bi: (bi, 0, 0, 0)),
            zspec(COL_KB), zspec(COL_VB), zspec(COL_OB),
            pl.BlockSpec((None, 1, GATE_COLS), lambda bi: (bi, 0, 0)),
            pl.BlockSpec((None, 1, DV_B), lambda bi: (layer, 0, 0)),
            pl.BlockSpec((None, None, HB, DK, DV_B), lambda bi: (layer, bi, 0, 0, 0)),
            pl.BlockSpec((None, None, HB, DK), lambda bi: (layer, bi, 0, 0)),
            pl.BlockSpec((None, None, HB, 128), lambda bi: (layer, bi, 0, 0)),
        ],
        out_specs=[
            pl.BlockSpec((None, 1, WB), lambda bi: (bi, 0, 0)),
            pl.BlockSpec((None, HB, DK, DV_B), lambda bi: (bi, 0, 0, 0)),
            pl.BlockSpec((None, HB, DK), lambda bi: (bi, 0, 0)),
            pl.BlockSpec((None, HB, 128), lambda bi: (bi, 0, 0)),
        ],
        out_shape=[
            jax.ShapeDtypeStruct((bs, 1, WB), F32),
            jax.ShapeDtypeStruct((bs, HB, DK, DV_B), F32),
            jax.ShapeDtypeStruct((bs, HB, DK), F32),
            jax.ShapeDtypeStruct((bs, HB, 128), F32),
        ],
        compiler_params=_cparams(1),
        name="mlstm_step",
    )(b_i, b_f, zs3, kcol, zs3, zs3, zs3, zgs3, norm_g, state_c, state_n, m0_lanes)


def _merge_out_kernel(a_ref, hb_ref, ga_ref, gb_ref, x_ref, wa_ref, wb_ref, wo_ref, o_ref, *, split):
    pa = _fold(jnp.dot(_lhs(a_ref[...], split), wa_ref[...], preferred_element_type=F32), split)
    pb = _fold(jnp.dot(_lhs(hb_ref[...], split), wb_ref[...], preferred_element_type=F32), split)
    merged = jax.nn.sigmoid(ga_ref[...].astype(F32)) * pa + jax.nn.sigmoid(gb_ref[...].astype(F32)) * pb
    o_ref[...] = x_ref[...] + _fold(jnp.dot(_lhs(merged, split), wo_ref[...], preferred_element_type=F32), split)


def merge_out(a, hb, z, x, w_a, w_b, w_o, layer, tm, split=False):
    m, d = x.shape
    once = pl.Buffered(1)
    return pl.pallas_call(
        functools.partial(_merge_out_kernel, split=split),
        grid=(m // tm,),
        in_specs=[
            pl.BlockSpec((tm, WA), lambda i: (i, 0)),
            pl.BlockSpec((tm, WB), lambda i: (i, 0)),
            pl.BlockSpec((tm, d), lambda i: (i, 0)),
            pl.BlockSpec((tm, d), lambda i: (i, 1)),
            pl.BlockSpec((tm, d), lambda i: (i, 0)),
            pl.BlockSpec((None, WA, d), lambda i: (layer, 0, 0), pipeline_mode=once),
            pl.BlockSpec((None, WB, d), lambda i: (layer, 0, 0), pipeline_mode=once),
            pl.BlockSpec((None, d, d), lambda i: (layer, 0, 0), pipeline_mode=once),
        ],
        out_specs=pl.BlockSpec((tm, d), lambda i: (i, 0)),
        out_shape=jax.ShapeDtypeStruct((m, d), F32),
        compiler_params=_cparams(1),
        name="merge_out",
    )(a, hb, z, z, x, w_a, w_b, w_o)


def _ffn_kernel(x_ref, g_ref, wg_ref, wu_ref, wd_ref, gf_ref, o_ref, h_ref, *, final_norm, split):
    f = pl.program_id(1)

    @pl.when(f == 0)
    def _():
        x = x_ref[...]
        h_ref[...] = _lhs(_rms(x, g_ref[...]), split)
        o_ref[...] = x

    h = h_ref[...]
    gate = _fold(jnp.dot(h, wg_ref[...], preferred_element_type=F32), split)
    up = _fold(jnp.dot(h, wu_ref[...], preferred_element_type=F32), split)
    act = _lhs(gate * jax.nn.sigmoid(gate) * up, split)
    o_ref[...] += _fold(jnp.dot(act, wd_ref[...], preferred_element_type=F32), split)

    if final_norm:
        @pl.when(f == pl.num_programs(1) - 1)
        def _():
            o_ref[...] = _rms(o_ref[...], gf_ref[...])


def ffn(x, g, w_gu, w_d, final_g, layer, tm, tf, final_norm, split=False):
    m, d = x.shape
    nf = D_FF // tf
    return pl.pallas_call(
        functools.partial(_ffn_kernel, final_norm=final_norm, split=split),
        grid=(m // tm, nf),
        in_specs=[
            pl.BlockSpec((tm, d), lambda i, f: (i, 0)),
            pl.BlockSpec((None, 1, d), lambda i, f: (layer, 0, 0)),
            pl.BlockSpec((None, d, tf), lambda i, f: (layer, 0, f)),
            pl.BlockSpec((None, d, tf), lambda i, f: (layer, 0, nf + f)),
            pl.BlockSpec((None, tf, d), lambda i, f: (layer, f, 0)),
            pl.BlockSpec((1, d), lambda i, f: (0, 0)),
        ],
        out_specs=pl.BlockSpec((tm, d), lambda i, f: (i, 0)),
        out_shape=jax.ShapeDtypeStruct((m, d), F32),
        scratch_shapes=[pltpu.VMEM((2 * tm if split else tm, d), BF16)],
        compiler_params=_cparams(2),
        name="ffn",
    )(x, g, w_gu, w_gu, w_d, final_g)


def kernel(x_prompt, x_sample, cache_k, cache_v, state_C, state_n, state_m, page_table, norm1_g, w_in, b_igate,
           b_fgate, lam_q1, lam_k1, lam_q2, lam_k2, subln_g, mlstm_norm_g, w_proj_a, w_proj_b, w_out, norm2_g,
           w_gu, w_down, final_g):
    bp, lp, d = x_prompt.shape
    bs = x_sample.shape[0]
    n_pool = cache_k.shape[1]
    n_pages = page_table.shape[1]
    past_len = n_pages * PAGE_SIZE
    n_rows = PAGE_SIZE * HA

    w_tail, w_head, w_gate = cast_w_in(jnp.swapaxes(w_in, 1, 2), tn=WA)
    w_a = w_proj_a.astype(BF16)
    w_b = w_proj_b.astype(BF16)
    w_o = w_out.astype(BF16)
    w_gu_b = w_gu.astype(BF16)
    w_d = w_down.astype(BF16)
    norm1 = norm1_g.reshape(DEPTH, 1, d)
    norm2 = norm2_g.reshape(DEPTH, 1, d)
    final_row = final_g.reshape(1, d)
    subln = subln_g.reshape(DEPTH, 1, DV_A)
    subln_col = subln_g.reshape(DEPTH, DV_A, 1)
    mnorm = mlstm_norm_g.reshape(DEPTH, 1, DV_B)

    slopes = jnp.asarray(2.0 ** (-8.0 * np.arange(1, HA + 1) / HA), dtype=F32)
    lam_all = (jnp.exp(jnp.sum(lam_q1 * lam_k1, axis=-1)) - jnp.exp(jnp.sum(lam_q2 * lam_k2, axis=-1))
               + jnp.asarray([0.8 - 0.6 * math.exp(-0.3 * l) for l in range(DEPTH)], dtype=F32))

    key_head = np.arange(n_rows) % HA
    key_tok = np.arange(n_rows) // HA
    q_head = np.arange(2 * HA) % HA
    slopes_np = 2.0 ** (-8.0 * np.arange(1, HA + 1) / HA)
    assert np.all(np.log2(slopes_np) == np.round(np.log2(slopes_np)))
    bias_np = np.where(key_head[None, :] == q_head[:, None], slopes_np[q_head][:, None] * key_tok[None, :], NEG)
    bias_tile = jnp.asarray(bias_np, dtype=F32)
    slope_col = jnp.asarray(slopes_np[q_head][:, None], dtype=F32)
    half_mask = jnp.asarray((np.arange(DV_A)[None, None, :] // QK_HALF) == np.arange(2)[:, None, None])

    cache_k4 = cache_k.reshape(DEPTH, n_pool, n_rows, DV_A)
    cache_v4 = cache_v.reshape(DEPTH, n_pool, n_rows, DV_A)
    pt_flat = page_table.reshape(-1)
    m0_lanes = jnp.broadcast_to(state_m[..., None], state_m.shape + (128,))

    xp = x_prompt.reshape(bp * lp, d)
    xs = x_sample.reshape(bs, d)
    outs = {k: [] for k in ("cp", "np", "mp", "ks", "vs", "cs", "ns", "ms")}
    kv_layers = []
    for l in range(DEPTH):
        lam = lam_all[l:l + 1]
        z, kv, zg = in_proj(xp, norm1, w_tail, w_head, w_gate, l, tm=TILES["in_proj_rows"], tn=WA)
        z3 = z.reshape(bp, lp, Z_COLS)
        kv_layers.append(kv)
        a = attn_prompt(z3, slopes, lam, subln_col, l, tq=TILES["attn_q"], tk=TILES["attn_q"] // 2)
        hb, c_p, n_p, m_p = mlstm_prompt(z3, zg.reshape(bp, lp, GATE_COLS), b_igate[l], b_fgate[l], mnorm, l,
                                         chunk=TILES["mlstm_chunk"])
        outs["cp"].append(c_p)
        outs["np"].append(n_p)
        outs["mp"].append(m_p[:, :, 0])
        xp = merge_out(a.reshape(bp * lp, WA), hb.reshape(bp * lp, WB), z, xp, w_a, w_b, w_o, l,
                       tm=TILES["merge_rows"])
        xp = ffn(xp, norm2, w_gu_b, w_d, final_row, l, tm=TILES["ffn_rows"], tf=TILES["ffn_cols"],
                 final_norm=l == DEPTH - 1)

        zs, kvs, zgs = in_proj(xs, norm1, w_tail, w_head, w_gate, l, tm=bs, tn=WA, split=True)
        k_new = kvs[:, :WA].reshape(bs, HA, DV_A)
        v_new = kvs[:, WA:].reshape(bs, HA, DV_A)
        outs["ks"].append(k_new.reshape(bs, 1, HA, DV_A))
        outs["vs"].append(v_new.reshape(bs, 1, HA, DV_A))
        q_new = zs[:, COL_QA:COL_QA + WA].reshape(bs, 1, HA, DV_A) * (QK_HALF ** -0.5)
        qm = jnp.where(half_mask[None], q_new, 0.0).reshape(bs, 2 * HA, DV_A).astype(BF16)
        kown = jnp.tile(k_new, (1, 2, 1))
        vown = jnp.tile(v_new, (1, 2, 1))
        a_s = attn_decode(qm, kown, vown, bias_tile, slope_col, lam, subln, cache_k4, cache_v4, pt_flat, l,
                          pages_per_step=TILES["decode_pages"])
        zs3 = zs.reshape(bs, 1, Z_COLS)
        kcol = zs[:, COL_KB:COL_KB + WB].reshape(bs, HB, DK, 1)
        hb_s, c_s, n_s, m_s = mlstm_step(zs3, kcol, zgs.reshape(bs, 1, GATE_COLS), b_igate[l], b_fgate[l], mnorm,
                                         state_C, state_n, m0_lanes, l)
        outs["cs"].append(c_s)
        outs["ns"].append(n_s)
        outs["ms"].append(m_s[:, :, 0])
        xs = merge_out(a_s.reshape(bs, WA), hb_s.reshape(bs, WB), zs, xs, w_a, w_b, w_o, l, tm=bs, split=True)
        xs = ffn(xs, norm2, w_gu_b, w_d, final_row, l, tm=bs, tf=TILES["ffn_cols"], final_norm=l == DEPTH - 1,
                 split=True)

    y_prompt = xp.reshape(bp, lp, d)
    y_sample = xs.reshape(bs, 1, d)
    st = {k: jnp.stack(v) for k, v in outs.items()}
    k_all, v_all = kv_layout(kv_layers, tm=TILES["kv_layout_rows"])
    st["kp"] = k_all.reshape(DEPTH, bp, lp, HA, DV_A)
    st["vp"] = v_all.reshape(DEPTH, bp, lp, HA, DV_A)
    return (y_prompt, y_sample, st["kp"], st["vp"], st["cp"], st["np"], st["mp"],
            st["ks"], st["vs"], st["cs"], st["ns"], st["ms"])
```

```python
import functools
import math

import numpy as np
import jax
import jax.numpy as jnp
from jax import lax
from jax.experimental import pallas as pl
from jax.experimental.pallas import tpu as pltpu

F32 = jnp.float32
BF16 = jnp.bfloat16

D_MODEL = 2048
DEPTH = 4
PAGE_SIZE = 128
WA = D_MODEL // 2
DV_A = 128
HA = WA // DV_A
QK_HALF = DV_A // 2
WB = D_MODEL // 2
HB = 4
DK = WB // HB
DV_B = WB // HB
D_FF = 5632
EPS = 1e-6
NEG = -1e30

Z_COLS = 2 * D_MODEL + 3 * WA + 4 * WB
COL_QA = 2 * D_MODEL
COL_KA = COL_QA + WA
COL_VA = COL_KA + WA
COL_QB = COL_VA + WA
COL_KB = COL_QB + WB
COL_VB = COL_KB + WB
COL_OB = COL_VB + WB
GATE_COLS = 128

VMEM_LIMIT = 56 * 1024 * 1024

TILES = dict(
    in_proj_rows=1024,
    attn_q=1024,
    mlstm_chunk=256,
    merge_rows=512,
    ffn_rows=1024,
    ffn_cols=512,
    decode_pages=16,
    kv_layout_rows=512,
)


def _cparams(n_axes, vmem=VMEM_LIMIT):
    return pltpu.CompilerParams(dimension_semantics=("arbitrary",) * n_axes, vmem_limit_bytes=vmem)


def _rms(x, g):
    return x * lax.rsqrt(jnp.mean(x * x, axis=-1, keepdims=True) + EPS) * g


def _lhs(x, split):
    if not split:
        return x.astype(BF16)
    x = x.astype(F32)
    hi = x.astype(BF16)
    return jnp.concatenate([hi, (x - hi.astype(F32)).astype(BF16)], axis=0)


def _fold(r, split):
    if not split:
        return r
    half = r.shape[0] // 2
    return r[:half] + r[half:]


def _cast_w_in_kernel(a_ref, b_ref, w_ref, wg_ref, *, n_tail, shift):
    j = pl.program_id(1)

    @pl.when(j < n_tail)
    def _():
        w_ref[...] = jnp.concatenate([a_ref[shift:, :], b_ref[...]], axis=0).astype(BF16)

    @pl.when(j == 0)
    def _():
        pad = jnp.zeros((wg_ref.shape[0] - shift, a_ref.shape[1]), F32)
        wg_ref[...] = jnp.concatenate([a_ref[:shift, :], pad], axis=0).astype(BF16)

    @pl.when(j >= n_tail)
    def _():
        w_ref[...] = a_ref[...].astype(BF16)


def cast_w_in(w_in_t, tn):
    n_layers, n_in, d = w_in_t.shape
    gate_lo = 3 * WA + 4 * WB
    shift = 2 * HB
    n_head, n_tail = gate_lo // tn, (n_in - gate_lo - shift) // tn
    assert gate_lo % tn == 0 and n_in == gate_lo + shift + n_tail * tn and tn % shift == 0 and shift == 8
    return pl.pallas_call(
        functools.partial(_cast_w_in_kernel, n_tail=n_tail, shift=shift),
        grid=(n_layers, n_tail + n_head),
        in_specs=[
            pl.BlockSpec((None, tn, d), lambda l, j: (l, jnp.where(j < n_tail, n_head + j, j - n_tail), 0)),
            pl.BlockSpec((None, shift, d),
                         lambda l, j: (l, jnp.where(j < n_tail, (n_head + j + 1) * (tn // shift), 0), 0)),
        ],
        out_specs=[
            pl.BlockSpec((None, tn, d), lambda l, j: (l, j, 0)),
            pl.BlockSpec((None, GATE_COLS, d), lambda l, j: (l, 0, 0)),
        ],
        out_shape=[jax.ShapeDtypeStruct((n_layers, n_tail * tn + gate_lo, d), BF16),
                   jax.ShapeDtypeStruct((n_layers, GATE_COLS, d), BF16)],
        compiler_params=_cparams(2),
        name="cast_w_in",
    )(w_in_t, w_in_t)


def _dot_nt(a, b):
    return lax.dot_general(a, b, (((1,), (1,)), ((), ())), preferred_element_type=F32)


def _inproj_kernel(x_ref, g_ref, w_ref, wg_ref, z_ref, kv_ref, zg_ref, h_ref, *, j_k, split):
    j = pl.program_id(1)

    @pl.when(j == 0)
    def _():
        h = _lhs(_rms(x_ref[...], g_ref[...]), split)
        h_ref[...] = h
        zg_ref[...] = _fold(_dot_nt(h, wg_ref[...]), split)

    r = _fold(_dot_nt(h_ref[...], w_ref[...]), split)
    z_ref[...] = r.astype(z_ref.dtype)

    @pl.when((j == j_k) | (j == j_k + 1))
    def _():
        kv_ref[...] = r


def in_proj(x, g, w_z, w_gate, layer, tm, tn, split=False):
    m, d = x.shape
    assert tn == WA and COL_VA == COL_KA + WA and w_z.shape[1] == Z_COLS
    j_k = COL_KA // tn
    return pl.pallas_call(
        functools.partial(_inproj_kernel, j_k=j_k, split=split),
        grid=(m // tm, Z_COLS // tn),
        in_specs=[
            pl.BlockSpec((tm, d), lambda i, j: (i, 0)),
            pl.BlockSpec((None, 1, d), lambda i, j: (layer, 0, 0)),
            pl.BlockSpec((None, tn, d), lambda i, j: (layer, j, 0)),
            pl.BlockSpec((None, GATE_COLS, d), lambda i, j: (layer, 0, 0)),
        ],
        out_specs=[
            pl.BlockSpec((tm, tn), lambda i, j: (i, j)),
            pl.BlockSpec((tm, WA), lambda i, j: (i, jnp.clip(j - j_k, 0, 1))),
            pl.BlockSpec((tm, GATE_COLS), lambda i, j: (i, 0)),
        ],
        out_shape=[jax.ShapeDtypeStruct((m, Z_COLS), F32 if split else BF16), jax.ShapeDtypeStruct((m, 2 * WA), F32),
                   jax.ShapeDtypeStruct((m, GATE_COLS), F32)],
        scratch_shapes=[pltpu.VMEM((2 * tm if split else tm, d), BF16)],
        compiler_params=_cparams(2),
        name="in_proj",
    )(x, g, w_z, w_gate)


def _kv_layout_kernel(*refs, tm):
    kv_refs, (ko_ref, vo_ref) = refs[:-2], refs[-2:]
    layer = pl.program_id(0)
    for a, kv_ref in enumerate(kv_refs):
        @pl.when(layer == a)
        def _(kv_ref=kv_ref):
            for h in range(HA):
                ko_ref[pl.ds(h, tm, stride=HA), :] = kv_ref[:, h * DV_A:(h + 1) * DV_A]
                vo_ref[pl.ds(h, tm, stride=HA), :] = kv_ref[:, WA + h * DV_A:WA + (h + 1) * DV_A]


def kv_layout(kvs, tm):
    n_layers = len(kvs)
    m = kvs[0].shape[0]
    n_i = m // tm

    def kv_spec(a):
        return pl.BlockSpec((tm, 2 * WA), lambda l, i: (jnp.where(l == a, i, jnp.where(l < a, 0, n_i - 1)), 0))

    out = jax.ShapeDtypeStruct((n_layers, m * HA, DV_A), F32)
    return pl.pallas_call(
        functools.partial(_kv_layout_kernel, tm=tm),
        grid=(n_layers, n_i),
        in_specs=[kv_spec(a) for a in range(n_layers)],
        out_specs=[pl.BlockSpec((None, tm * HA, DV_A), lambda l, i: (l, i, 0))] * 2,
        out_shape=[out, out],
        compiler_params=_cparams(2),
        name="kv_layout",
    )(*kvs)


def _attn_prompt_kernel(slope_ref, lam_ref, q_ref, k_ref, v_ref, qfeat_ref, kfeat_ref, g_ref, o_ref,
                        kb_ref, vt_ref, s_ref, m_ref, acc_ref, *, tq, tk, out_scale):
    h = pl.program_id(1)
    n_blocks = vt_ref.shape[0]
    n_sub = tq // tk

    kb_ref[:, :DV_A] = k_ref[...].astype(BF16)
    kb_ref[:, DV_A:] = kfeat_ref[...]
    for jb in range(n_blocks):
        vt_ref[jb, :DV_A, :] = v_ref[jb * tk:(jb + 1) * tk, :].astype(F32).T.astype(BF16)
        vt_ref[jb, DV_A:, :] = jnp.ones((vt_ref.shape[1] - DV_A, tk), BF16)

    slope = slope_ref[h]
    lam = lam_ref[0]

    def q_tile(qi, carry):
        q_rows = pl.ds(pl.multiple_of(qi * tq, tq), tq)
        q = q_ref[q_rows, :] * jnp.asarray(QK_HALF ** -0.5, q_ref.dtype)
        lane = lax.broadcasted_iota(jnp.int32, (tq, DV_A), 1)
        qs = jnp.concatenate([jnp.where(lane < QK_HALF, q, 0), jnp.where(lane >= QK_HALF, q, 0)], axis=0)
        qaug = jnp.concatenate([qs.astype(BF16), qfeat_ref[...]], axis=1)

        def segments(u):
            return [(c * tq + u * tk, tq - u * tk) for c in range(2)]

        def scores(jb, segs=((0, 2 * tq),)):
            kj = kb_ref[pl.ds(pl.multiple_of(jb * tk, tk), tk), :]
            qa = jnp.concatenate([qaug[a:a + w] for a, w in segs], axis=0)
            return lax.dot_general(kj, qa, (((1,), (1,)), ((), ())), preferred_element_type=F32)

        def softmax_pv(jb, s, off, segs=((0, 2 * tq),)):
            m = jnp.concatenate([m_ref[:, a:a + w] for a, w in segs], axis=1)
            m_new = jnp.maximum(m, jnp.max(s, axis=0, keepdims=True) - off)
            alpha = jnp.exp(m - m_new)
            p = jnp.exp(s - (m_new + off))
            pv = jnp.dot(vt_ref[jb], p.astype(BF16), preferred_element_type=F32)
            pos = 0
            for a, w in segs:
                acc_ref[:, a:a + w] = alpha[:, pos:pos + w] * acc_ref[:, a:a + w] + pv[:, pos:pos + w]
                m_ref[:, a:a + w] = m_new[:, pos:pos + w]
                pos += w

        def causal(s, width):
            krow = lax.broadcasted_iota(jnp.int32, s.shape, 0)
            qcol = jnp.bitwise_and(lax.broadcasted_iota(jnp.int32, s.shape, 1), width - 1)
            return jnp.where(krow <= qcol, s, NEG)

        def full_blocks(jj, carry):
            for u in range(n_sub):
                jb = jj * n_sub + u
                s_ref[(u + 1) % 2] = scores(jb + 1)
                softmax_pv(jb, s_ref[u % 2], slope * (qi * tq - jb * tk).astype(F32))
            return carry

        s_ref[0] = scores(0)
        m_ref[...] = jnp.full(m_ref.shape, NEG, F32)
        acc_ref[...] = jnp.zeros(acc_ref.shape, F32)
        lax.fori_loop(0, qi, full_blocks, 0)
        for u in range(n_sub):
            jb = qi * n_sub + u
            width = tq - u * tk
            if u + 1 < n_sub:
                s_ref[(u + 1) % 2, :, :2 * (width - tk)] = scores(jb + 1, segments(u + 1))
            softmax_pv(jb, causal(s_ref[u % 2, :, :2 * width], width), slope * (-u * tk), segments(u))
        o = acc_ref[:DV_A, :] / acc_ref[DV_A:DV_A + 1, :]
        a = o[:, :tq] - lam * o[:, tq:]
        y = a * lax.rsqrt(jnp.mean(a * a, axis=0, keepdims=True) + EPS) * g_ref[...] * out_scale
        o_ref[q_rows, :] = y.T.astype(o_ref.dtype)
        return carry

    lax.fori_loop(0, q_ref.shape[0] // tq, q_tile, 0)


def attn_prompt(z3, slopes, lam, subln_col, layer, tq, tk):
    b, seq, _ = z3.shape
    assert tk <= 512 and tq == 2 * tk and tq & (tq - 1) == 0 and tk & (tk - 1) == 0
    lam_init = 0.8 - 0.6 * math.exp(-0.3 * layer)
    kern = functools.partial(_attn_prompt_kernel, tq=tq, tk=tk, out_scale=1.0 - lam_init)
    slopes_np = 2.0 ** (-8.0 * np.arange(1, HA + 1) / HA)
    assert np.all(np.log2(slopes_np) == np.round(np.log2(slopes_np)))
    qpos = np.arange(2 * tq) % tq
    qfeat = np.zeros((HA, 2 * tq, DV_A), np.float32)
    qfeat[:, :, 0] = slopes_np[:, None]
    qfeat[:, :, 1] = -slopes_np[:, None] * (qpos % 256)[None, :]
    qfeat[:, :, 2] = -slopes_np[:, None] * (qpos - qpos % 256)[None, :]
    qfeat[:, :, 3] = slopes_np[:, None]
    kpos = np.arange(seq) % tk
    kfeat = np.zeros((seq, DV_A), np.float32)
    kfeat[:, 0] = kpos % 256
    kfeat[:, 1:3] = 1.0
    kfeat[:, 3] = kpos - kpos % 256
    return pl.pallas_call(
        kern,
        grid=(b, HA),
        in_specs=[
            pl.BlockSpec(memory_space=pltpu.SMEM),
            pl.BlockSpec(memory_space=pltpu.SMEM),
            pl.BlockSpec((None, seq, DV_A), lambda bi, h: (bi, 0, COL_QA // DV_A + h)),
            pl.BlockSpec((None, seq, DV_A), lambda bi, h: (bi, 0, COL_KA // DV_A + h)),
            pl.BlockSpec((None, seq, DV_A), lambda bi, h: (bi, 0, COL_VA // DV_A + h)),
            pl.BlockSpec((None, 2 * tq, DV_A), lambda bi, h: (h, 0, 0)),
            pl.BlockSpec((seq, DV_A), lambda bi, h: (0, 0)),
            pl.BlockSpec((None, DV_A, 1), lambda bi, h: (layer, 0, 0)),
        ],
        out_specs=pl.BlockSpec((None, seq, DV_A), lambda bi, h: (bi, 0, h)),
        out_shape=jax.ShapeDtypeStruct((b, seq, WA), BF16),
        scratch_shapes=[pltpu.VMEM((seq, 2 * DV_A), BF16), pltpu.VMEM((seq // tk, DV_A + 16, tk), BF16),
                        pltpu.VMEM((2, tk, 2 * tq), F32), pltpu.VMEM((1, 2 * tq), F32),
                        pltpu.VMEM((DV_A + 16, 2 * tq), F32)],
        compiler_params=_cparams(2),
        name="attn_prompt",
    )(slopes, lam, z3, z3, z3, jnp.asarray(qfeat, BF16), jnp.asarray(kfeat, BF16), subln_col)


def _attn_decode_kernel(pt_ref, lam_ref, qm_ref, kown_ref, vown_ref, bias_ref, slope_ref, g_ref, *rest,
                        pages_per_step, past_len, out_scale):
    del pt_ref
    k_refs = rest[:pages_per_step]
    v_refs = rest[pages_per_step:2 * pages_per_step]
    o_ref, m_ref, l_ref, acc_ref = rest[2 * pages_per_step:]
    i = pl.program_id(1)
    qm = qm_ref[...]

    @pl.when(i == 0)
    def _():
        ko = kown_ref[...].astype(BF16).astype(F32)
        m_ref[...] = jnp.sum(qm.astype(F32) * ko, axis=-1, keepdims=True)
        l_ref[...] = jnp.ones_like(l_ref)
        acc_ref[...] = vown_ref[...].astype(BF16).astype(F32)

    ss = []
    for gi in range(pages_per_step):
        kp = k_refs[gi][...].astype(BF16)
        s = lax.dot_general(qm, kp, (((1,), (1,)), ((), ())), preferred_element_type=F32)
        page_pos = i * pages_per_step + gi
        off = (page_pos * PAGE_SIZE - past_len).astype(F32)
        ss.append(s + (bias_ref[...] + slope_ref[...] * off))
    s = jnp.concatenate(ss, axis=-1)
    m_old = m_ref[...]
    m_new = jnp.maximum(m_old, jnp.max(s, axis=-1, keepdims=True))
    alpha = jnp.exp(m_old - m_new)
    p = jnp.exp(s - m_new)
    l_ref[...] = alpha * l_ref[...] + jnp.sum(p, axis=-1, keepdims=True)
    pv = jnp.zeros(acc_ref.shape, F32)
    n_rows = PAGE_SIZE * HA
    for gi in range(pages_per_step):
        vp = v_refs[gi][...].astype(BF16)
        pv = pv + jnp.dot(p[:, gi * n_rows:(gi + 1) * n_rows].astype(BF16), vp, preferred_element_type=F32)
    acc_ref[...] = alpha * acc_ref[...] + pv
    m_ref[...] = m_new

    @pl.when(i == pl.num_programs(1) - 1)
    def _():
        o = acc_ref[...] / l_ref[...]
        a = o[:HA] - lam_ref[0] * o[HA:]
        o_ref[...] = (_rms(a, g_ref[...]) * out_scale).astype(o_ref.dtype)


def attn_decode(qm, kown, vown, bias_tile, slope_col, lam, subln_g, cache_k4, cache_v4, page_table_flat, layer,
                pages_per_step):
    bs = qm.shape[0]
    n_pages = page_table_flat.shape[0] // bs
    n_rows = PAGE_SIZE * HA
    lam_init = 0.8 - 0.6 * math.exp(-0.3 * layer)
    kern = functools.partial(_attn_decode_kernel, pages_per_step=pages_per_step,
                             past_len=n_pages * PAGE_SIZE, out_scale=1.0 - lam_init)

    def page_spec(gi):
        return pl.BlockSpec((None, None, n_rows, DV_A),
                            lambda b, i, pt: (layer, pt[b * n_pages + i * pages_per_step + gi], 0, 0))

    grid_spec = pltpu.PrefetchScalarGridSpec(
        num_scalar_prefetch=1,
        grid=(bs, n_pages // pages_per_step),
        in_specs=[
            pl.BlockSpec(memory_space=pltpu.SMEM),
            pl.BlockSpec((None, 2 * HA, DV_A), lambda b, i, pt: (b, 0, 0)),
            pl.BlockSpec((None, 2 * HA, DV_A), lambda b, i, pt: (b, 0, 0)),
            pl.BlockSpec((None, 2 * HA, DV_A), lambda b, i, pt: (b, 0, 0)),
            pl.BlockSpec((2 * HA, n_rows), lambda b, i, pt: (0, 0)),
            pl.BlockSpec((2 * HA, 1), lambda b, i, pt: (0, 0)),
            pl.BlockSpec((None, 1, DV_A), lambda b, i, pt: (layer, 0, 0)),
        ] + [page_spec(gi) for gi in range(pages_per_step)] * 2,
        out_specs=pl.BlockSpec((None, HA, DV_A), lambda b, i, pt: (b, 0, 0)),
        scratch_shapes=[pltpu.VMEM((2 * HA, 1), F32), pltpu.VMEM((2 * HA, 1), F32),
                        pltpu.VMEM((2 * HA, DV_A), F32)],
    )
    return pl.pallas_call(
        kern,
        grid_spec=grid_spec,
        out_shape=jax.ShapeDtypeStruct((bs, HA, DV_A), F32),
        compiler_params=_cparams(2),
        name="attn_decode",
    )(page_table_flat, lam, qm, kown, vown, bias_tile, slope_col, subln_g,
      *([cache_k4] * pages_per_step), *([cache_v4] * pages_per_step))


def _log_sigmoid(x):
    return jnp.minimum(x, 0.0) - jnp.log1p(jnp.exp(-jnp.abs(x)))


def _mlstm_prompt_kernel(bi_ref, bf_ref, q_ref, k_ref, v_ref, ob_ref, zg_ref, g_ref,
                         hb_ref, c_out_ref, n_out_ref, m_out_ref, c_ref, n_ref, m_ref, *, chunk):
    ci = pl.program_id(1)

    @pl.when(ci == 0)
    def _():
        c_ref[...] = jnp.zeros_like(c_ref)
        n_ref[...] = jnp.zeros_like(n_ref)
        m_ref[...] = jnp.zeros_like(m_ref)

    lane_g = lax.broadcasted_iota(jnp.int32, (1, GATE_COLS), 1)
    gate_bias = jnp.zeros((1, GATE_COLS), F32)
    for h in range(HB):
        gate_bias = jnp.where(lane_g == h, bi_ref[h], jnp.where(lane_g == HB + h, bf_ref[h], gate_bias))
    pre = zg_ref[...] + gate_bias
    gates = jnp.where(lane_g < HB, pre, _log_sigmoid(pre))
    gates_t = gates.T
    row = lax.broadcasted_iota(jnp.int32, (chunk, chunk), 0)
    col = lax.broadcasted_iota(jnp.int32, (chunk, chunk), 1)
    causal = col <= row
    g_norm = g_ref[...]
    q_scale = jnp.asarray(DK ** -0.5, q_ref.dtype)
    heads = range(HB)
    sls = [slice(h * DK, (h + 1) * DK) for h in heads]
    qs = [(q_ref[:, sl] * q_scale).astype(BF16) for sl in sls]
    kb = [k_ref[:, sl].astype(BF16) for sl in sls]
    vb = [v_ref[:, sl].astype(BF16) for sl in sls]
    c_old = [c_ref[h] for h in heads]
    n_old = [n_ref[h:h + 1, :] for h in heads]
    s_raw = [lax.dot_general(qs[h], kb[h], (((1,), (1,)), ((), ())), preferred_element_type=F32) for h in heads]
    inter = [jnp.dot(qs[h], c_old[h].astype(BF16), preferred_element_type=F32) for h in heads]

    b_col, m_t, w_inter, wmat = [], [], [], []
    for h in heads:
        i_row = gates_t[h:h + 1, :]
        lf_col = gates[:, HB + h:HB + h + 1]
        lf_row = gates_t[HB + h:HB + h + 1, :]
        bc = jnp.sum(jnp.where(causal, lf_row, 0.0), axis=1, keepdims=True)
        br = jnp.sum(jnp.where(row <= col, lf_col, 0.0), axis=0, keepdims=True)
        g_col = bc + m_ref[h:h + 1, 0:1]
        dmat = jnp.where(causal, bc - br + i_row, NEG)
        mt = jnp.maximum(g_col, jnp.max(dmat, axis=1, keepdims=True))
        b_col.append(bc)
        m_t.append(mt)
        w_inter.append(jnp.exp(g_col - mt))
        wmat.append(jnp.exp(dmat - mt))

    s = [s_raw[h] * wmat[h] for h in heads]
    sv = [jnp.dot(s[h].astype(BF16), vb[h], preferred_element_type=F32) for h in heads]

    kw = []
    for h in heads:
        m_last = m_t[h][chunk - 1:chunk, :]
        wc_col = jnp.exp(b_col[h][chunk - 1:chunk, :] - b_col[h] + gates[:, h:h + 1] - m_last)
        kw.append(kb[h].astype(F32) * wc_col.astype(BF16).astype(F32))
    kv_new = [jnp.dot(kw[h].T.astype(BF16), vb[h], preferred_element_type=F32) for h in heads]

    for h in heads:
        num = w_inter[h] * inter[h] + sv[h]
        qn = jnp.sum(qs[h].astype(F32) * n_old[h].astype(BF16).astype(F32), axis=-1, keepdims=True)
        den = w_inter[h] * qn + jnp.sum(s[h], axis=-1, keepdims=True)
        hv = num / jnp.maximum(jnp.abs(den), jnp.exp(-m_t[h]))
        hb = _rms(hv, g_norm) * jax.nn.sigmoid(ob_ref[:, sls[h]].astype(F32))
        hb_ref[:, sls[h]] = hb.astype(hb_ref.dtype)
        w_last = w_inter[h][chunk - 1:chunk, :]
        c_ref[h] = w_last * c_old[h] + kv_new[h]
        n_ref[h:h + 1, :] = w_last * n_old[h] + jnp.sum(kw[h], axis=0, keepdims=True)
        m_ref[h:h + 1, :] = jnp.broadcast_to(m_t[h][chunk - 1:chunk, :], (1, m_ref.shape[1]))

    @pl.when(ci == pl.num_programs(1) - 1)
    def _():
        c_out_ref[...] = c_ref[...]
        n_out_ref[...] = n_ref[...]
        m_out_ref[...] = m_ref[...]


def mlstm_prompt(z3, zg3, b_i, b_f, norm_g, layer, chunk):
    b, seq, _ = z3.shape

    def zspec(col):
        return pl.BlockSpec((None, chunk, WB), lambda bi, ci: (bi, ci, col // WB))

    return pl.pallas_call(
        functools.partial(_mlstm_prompt_kernel, chunk=chunk),
        grid=(b, seq // chunk),
        in_specs=[
            pl.BlockSpec(memory_space=pltpu.SMEM),
            pl.BlockSpec(memory_space=pltpu.SMEM),
            zspec(COL_QB), zspec(COL_KB), zspec(COL_VB), zspec(COL_OB),
            pl.BlockSpec((None, chunk, GATE_COLS), lambda bi, ci: (bi, ci, 0)),
            pl.BlockSpec((None, 1, DV_B), lambda bi, ci: (layer, 0, 0)),
        ],
        out_specs=[
            pl.BlockSpec((None, chunk, WB), lambda bi, ci: (bi, ci, 0)),
            pl.BlockSpec((None, HB, DK, DV_B), lambda bi, ci: (bi, 0, 0, 0)),
            pl.BlockSpec((None, HB, DK), lambda bi, ci: (bi, 0, 0)),
            pl.BlockSpec((None, HB, 128), lambda bi, ci: (bi, 0, 0)),
        ],
        out_shape=[
            jax.ShapeDtypeStruct((b, seq, WB), BF16),
            jax.ShapeDtypeStruct((b, HB, DK, DV_B), F32),
            jax.ShapeDtypeStruct((b, HB, DK), F32),
            jax.ShapeDtypeStruct((b, HB, 128), F32),
        ],
        scratch_shapes=[pltpu.VMEM((HB, DK, DV_B), F32), pltpu.VMEM((HB, DK), F32), pltpu.VMEM((HB, 128), F32)],
        compiler_params=_cparams(2),
        name="mlstm_prompt",
    )(b_i, b_f, z3, z3, z3, z3, zg3, norm_g)


def _mlstm_step_kernel(bi_ref, bf_ref, q_ref, kcol_ref, krow_ref, v_ref, ob_ref, zg_ref, g_ref,
                       c0_ref, n0_ref, m0_ref, hb_ref, c_out_ref, n_out_ref, m_out_ref):
    gates = zg_ref[...]
    g_norm = g_ref[...]
    for h in range(HB):
        sl = slice(h * DK, (h + 1) * DK)
        i_pre = gates[:, h:h + 1] + bi_ref[h]
        lf = _log_sigmoid(gates[:, HB + h:HB + h + 1] + bf_ref[h])
        m0 = m0_ref[h:h + 1, 0:1]
        g_st = lf + m0
        m_t = jnp.maximum(g_st, i_pre)
        w_inter = jnp.exp(g_st - m_t)
        w_in = jnp.exp(i_pre - m_t)
        qs = (q_ref[:, sl] * (DK ** -0.5)).astype(BF16)
        k_row = krow_ref[:, sl].astype(BF16).astype(F32)
        k_col = kcol_ref[h].astype(BF16).astype(F32)
        v_row = v_ref[:, sl].astype(BF16).astype(F32)
        c_old = c0_ref[h]
        n_old = n0_ref[h:h + 1, :]
        qf = qs.astype(F32)
        s = jnp.sum(qf * k_row, axis=-1, keepdims=True) * w_in
        q16 = jnp.broadcast_to(qs, (16, DK))
        qc = jnp.dot(q16, c_old.astype(BF16), preferred_element_type=F32)[0:1, :]
        num = w_inter * qc + s.astype(BF16).astype(F32) * v_row
        qn = jnp.sum(qf * n_old.astype(BF16).astype(F32), axis=-1, keepdims=True)
        den = w_inter * qn + s
        hv = num / jnp.maximum(jnp.abs(den), jnp.exp(-m_t))
        hb = _rms(hv, g_norm) * jax.nn.sigmoid(ob_ref[:, sl])
        hb_ref[:, sl] = hb.astype(hb_ref.dtype)
        wb = w_in.astype(BF16).astype(F32)
        kw_col = (k_col * wb).astype(BF16).astype(F32)
        c_out_ref[h] = w_inter * c_old + kw_col * v_row
        n_out_ref[h:h + 1, :] = w_inter * n_old + wb * k_row
        m_out_ref[h:h + 1, :] = jnp.broadcast_to(m_t, (1, m_out_ref.shape[1]))


def mlstm_step(zs3, kcol, zgs3, b_i, b_f, norm_g, state_c, state_n, m0_lanes, layer):
    bs = zs3.shape[0]

    def zspec(col):
        return pl.BlockSpec((None, 1, WB), lambda bi: (bi, 0, col // WB))

    return pl.pallas_call(
        _mlstm_step_kernel,
        grid=(bs,),
        in_specs=[
            pl.BlockSpec(memory_space=pltpu.SMEM),
            pl.BlockSpec(memory_space=pltpu.SMEM),
            zspec(COL_QB),
            pl.BlockSpec((None, HB, DK, 1), lambda bi: (bi, 0, 0, 0)),
            zspec(COL_KB), zspec(COL_VB), zspec(COL_OB),
            pl.BlockSpec((None, 1, GATE_COLS), lambda bi: (bi, 0, 0)),
            pl.BlockSpec((None, 1, DV_B), lambda bi: (layer, 0, 0)),
            pl.BlockSpec((None, None, HB, DK, DV_B), lambda bi: (layer, bi, 0, 0, 0)),
            pl.BlockSpec((None, None, HB, DK), lambda bi: (layer, bi, 0, 0)),
            pl.BlockSpec((None, None, HB, 128), lambda bi: (layer, bi, 0, 0)),
        ],
        out_specs=[
            pl.BlockSpec((None, 1, WB), lambda bi: (bi, 0, 0)),
            pl.BlockSpec((None, HB, DK, DV_B), lambda bi: (bi, 0, 0, 0)),
            pl.BlockSpec((None, HB, DK), lambda bi: (bi, 0, 0)),
            pl.BlockSpec((None, HB, 128), lambda bi: (bi, 0, 0)),
        ],
        out_shape=[
            jax.ShapeDtypeStruct((bs, 1, WB), F32),
            jax.ShapeDtypeStruct((bs, HB, DK, DV_B), F32),
            jax.ShapeDtypeStruct((bs, HB, DK), F32),
            jax.ShapeDtypeStruct((bs, HB, 128), F32),
        ],
        compiler_params=_cparams(1),
        name="mlstm_step",
    )(b_i, b_f, zs3, kcol, zs3, zs3, zs3, zgs3, norm_g, state_c, state_n, m0_lanes)


def _merge_out_kernel(a_ref, hb_ref, ga_ref, gb_ref, x_ref, wa_ref, wb_ref, wo_ref, o_ref, *, split):
    pa = _fold(jnp.dot(_lhs(a_ref[...], split), wa_ref[...], preferred_element_type=F32), split)
    pb = _fold(jnp.dot(_lhs(hb_ref[...], split), wb_ref[...], preferred_element_type=F32), split)
    merged = jax.nn.sigmoid(ga_ref[...].astype(F32)) * pa + jax.nn.sigmoid(gb_ref[...].astype(F32)) * pb
    o_ref[...] = x_ref[...] + _fold(jnp.dot(_lhs(merged, split), wo_ref[...], preferred_element_type=F32), split)


def merge_out(a, hb, z, x, w_a, w_b, w_o, layer, tm, split=False):
    m, d = x.shape
    once = pl.Buffered(1)
    return pl.pallas_call(
        functools.partial(_merge_out_kernel, split=split),
        grid=(m // tm,),
        in_specs=[
            pl.BlockSpec((tm, WA), lambda i: (i, 0)),
            pl.BlockSpec((tm, WB), lambda i: (i, 0)),
            pl.BlockSpec((tm, d), lambda i: (i, 0)),
            pl.BlockSpec((tm, d), lambda i: (i, 1)),
            pl.BlockSpec((tm, d), lambda i: (i, 0)),
            pl.BlockSpec((None, WA, d), lambda i: (layer, 0, 0), pipeline_mode=once),
            pl.BlockSpec((None, WB, d), lambda i: (layer, 0, 0), pipeline_mode=once),
            pl.BlockSpec((None, d, d), lambda i: (layer, 0, 0), pipeline_mode=once),
        ],
        out_specs=pl.BlockSpec((tm, d), lambda i: (i, 0)),
        out_shape=jax.ShapeDtypeStruct((m, d), F32),
        compiler_params=_cparams(1),
        name="merge_out",
    )(a, hb, z, z, x, w_a, w_b, w_o)


def _ffn_kernel(x_ref, g_ref, wg_ref, wu_ref, wd_ref, gf_ref, o_ref, h_ref, *, final_norm, split):
    f = pl.program_id(1)

    @pl.when(f == 0)
    def _():
        x = x_ref[...]
        h_ref[...] = _lhs(_rms(x, g_ref[...]), split)
        o_ref[...] = x

    h = h_ref[...]
    gate = _fold(jnp.dot(h, wg_ref[...], preferred_element_type=F32), split)
    up = _fold(jnp.dot(h, wu_ref[...], preferred_element_type=F32), split)
    act = _lhs(gate * jax.nn.sigmoid(gate) * up, split)
    o_ref[...] += _fold(jnp.dot(act, wd_ref[...], preferred_element_type=F32), split)

    if final_norm:
        @pl.when(f == pl.num_programs(1) - 1)
        def _():
            o_ref[...] = _rms(o_ref[...], gf_ref[...])


def ffn(x, g, w_gu, w_d, final_g, layer, tm, tf, final_norm, split=False):
    m, d = x.shape
    nf = D_FF // tf
    return pl.pallas_call(
        functools.partial(_ffn_kernel, final_norm=final_norm, split=split),
        grid=(m // tm, nf),
        in_specs=[
            pl.BlockSpec((tm, d), lambda i, f: (i, 0)),
            pl.BlockSpec((None, 1, d), lambda i, f: (layer, 0, 0)),
            pl.BlockSpec((None, d, tf), lambda i, f: (layer, 0, f)),
            pl.BlockSpec((None, d, tf), lambda i, f: (layer, 0, nf + f)),
            pl.BlockSpec((None, tf, d), lambda i, f: (layer, f, 0)),
            pl.BlockSpec((1, d), lambda i, f: (0, 0)),
        ],
        out_specs=pl.BlockSpec((tm, d), lambda i, f: (i, 0)),
        out_shape=jax.ShapeDtypeStruct((m, d), F32),
        scratch_shapes=[pltpu.VMEM((2 * tm if split else tm, d), BF16)],
        compiler_params=_cparams(2),
        name="ffn",
    )(x, g, w_gu, w_gu, w_d, final_g)


def kernel(x_prompt, x_sample, cache_k, cache_v, state_C, state_n, state_m, page_table, norm1_g, w_in, b_igate,
           b_fgate, lam_q1, lam_k1, lam_q2, lam_k2, subln_g, mlstm_norm_g, w_proj_a, w_proj_b, w_out, norm2_g,
           w_gu, w_down, final_g):
    bp, lp, d = x_prompt.shape
    bs = x_sample.shape[0]
    n_pool = cache_k.shape[1]
    n_pages = page_table.shape[1]
    past_len = n_pages * PAGE_SIZE
    n_rows = PAGE_SIZE * HA

    w_z, w_gate = cast_w_in(jnp.swapaxes(w_in, 1, 2), tn=WA)
    w_a = w_proj_a.astype(BF16)
    w_b = w_proj_b.astype(BF16)
    w_o = w_out.astype(BF16)
    w_gu_b = w_gu.astype(BF16)
    w_d = w_down.astype(BF16)
    norm1 = norm1_g.reshape(DEPTH, 1, d)
    norm2 = norm2_g.reshape(DEPTH, 1, d)
    final_row = final_g.reshape(1, d)
    subln = subln_g.reshape(DEPTH, 1, DV_A)
    subln_col = subln_g.reshape(DEPTH, DV_A, 1)
    mnorm = mlstm_norm_g.reshape(DEPTH, 1, DV_B)

    slopes = jnp.asarray(2.0 ** (-8.0 * np.arange(1, HA + 1) / HA), dtype=F32)
    lam_all = (jnp.exp(jnp.sum(lam_q1 * lam_k1, axis=-1)) - jnp.exp(jnp.sum(lam_q2 * lam_k2, axis=-1))
               + jnp.asarray([0.8 - 0.6 * math.exp(-0.3 * l) for l in range(DEPTH)], dtype=F32))

    key_head = np.arange(n_rows) % HA
    key_tok = np.arange(n_rows) // HA
    q_head = np.arange(2 * HA) % HA
    slopes_np = 2.0 ** (-8.0 * np.arange(1, HA + 1) / HA)
    assert np.all(np.log2(slopes_np) == np.round(np.log2(slopes_np)))
    bias_np = np.where(key_head[None, :] == q_head[:, None], slopes_np[q_head][:, None] * key_tok[None, :], NEG)
    bias_tile = jnp.asarray(bias_np, dtype=F32)
    slope_col = jnp.asarray(slopes_np[q_head][:, None], dtype=F32)
    half_mask = jnp.asarray((np.arange(DV_A)[None, None, :] // QK_HALF) == np.arange(2)[:, None, None])

    cache_k4 = cache_k.reshape(DEPTH, n_pool, n_rows, DV_A)
    cache_v4 = cache_v.reshape(DEPTH, n_pool, n_rows, DV_A)
    pt_flat = page_table.reshape(-1)
    m0_lanes = jnp.broadcast_to(state_m[..., None], state_m.shape + (128,))

    xp = x_prompt.reshape(bp * lp, d)
    xs = x_sample.reshape(bs, d)
    outs = {k: [] for k in ("cp", "np", "mp", "ks", "vs", "cs", "ns", "ms")}
    kv_layers = []
    for l in range(DEPTH):
        lam = lam_all[l:l + 1]
        z, kv, zg = in_proj(xp, norm1, w_z, w_gate, l, tm=TILES["in_proj_rows"], tn=WA)
        z3 = z.reshape(bp, lp, Z_COLS)
        kv_layers.append(kv)
        a = attn_prompt(z3, slopes, lam, subln_col, l, tq=TILES["attn_q"], tk=TILES["attn_q"] // 2)
        hb, c_p, n_p, m_p = mlstm_prompt(z3, zg.reshape(bp, lp, GATE_COLS), b_igate[l], b_fgate[l], mnorm, l,
                                         chunk=TILES["mlstm_chunk"])
        outs["cp"].append(c_p)
        outs["np"].append(n_p)
        outs["mp"].append(m_p[:, :, 0])
        xp = merge_out(a.reshape(bp * lp, WA), hb.reshape(bp * lp, WB), z, xp, w_a, w_b, w_o, l,
                       tm=TILES["merge_rows"])
        xp = ffn(xp, norm2, w_gu_b, w_d, final_row, l, tm=TILES["ffn_rows"], tf=TILES["ffn_cols"],
                 final_norm=l == DEPTH - 1)

        zs, kvs, zgs = in_proj(xs, norm1, w_z, w_gate, l, tm=bs, tn=WA, split=True)
        k_new = kvs[:, :WA].reshape(bs, HA, DV_A)
        v_new = kvs[:, WA:].reshape(bs, HA, DV_A)
        outs["ks"].append(k_new.reshape(bs, 1, HA, DV_A))
        outs["vs"].append(v_new.reshape(bs, 1, HA, DV_A))
        q_new = zs[:, COL_QA:COL_QA + WA].reshape(bs, 1, HA, DV_A) * (QK_HALF ** -0.5)
        qm = jnp.where(half_mask[None], q_new, 0.0).reshape(bs, 2 * HA, DV_A).astype(BF16)
        kown = jnp.tile(k_new, (1, 2, 1))
        vown = jnp.tile(v_new, (1, 2, 1))
        a_s = attn_decode(qm, kown, vown, bias_tile, slope_col, lam, subln, cache_k4, cache_v4, pt_flat, l,
                          pages_per_step=TILES["decode_pages"])
        zs3 = zs.reshape(bs, 1, Z_COLS)
        kcol = zs[:, COL_KB:COL_KB + WB].reshape(bs, HB, DK, 1)
        hb_s, c_s, n_s, m_s = mlstm_step(zs3, kcol, zgs.reshape(bs, 1, GATE_COLS), b_igate[l], b_fgate[l], mnorm,
                                         state_C, state_n, m0_lanes, l)
        outs["cs"].append(c_s)
        outs["ns"].append(n_s)
        outs["ms"].append(m_s[:, :, 0])
        xs = merge_out(a_s.reshape(bs, WA), hb_s.reshape(bs, WB), zs, xs, w_a, w_b, w_o, l, tm=bs, split=True)
        xs = ffn(xs, norm2, w_gu_b, w_d, final_row, l, tm=bs, tf=TILES["ffn_cols"], final_norm=l == DEPTH - 1,
                 split=True)

    y_prompt = xp.reshape(bp, lp, d)
    y_sample = xs.reshape(bs, 1, d)
    st = {k: jnp.stack(v) for k, v in outs.items()}
    k_all, v_all = kv_layout(kv_layers, tm=TILES["kv_layout_rows"])
    st["kp"] = k_all.reshape(DEPTH, bp, lp, HA, DV_A)
    st["vp"] = v_all.reshape(DEPTH, bp, lp, HA, DV_A)
    return (y_prompt, y_sample, st["kp"], st["vp"], st["cp"], st["np"], st["mp"],
            st["ks"], st["vs"], st["cs"], st["ns"], st["ms"])
```
